```python
import math
import jax, jax.numpy as jnp
from jax import lax
import numpy as np

D_MODEL = 1024
BATCH = 2
SEQ = 8192
DEPTH = 4
DEC_BATCH = 32
DEC_SEQ = 1
PAST_LEN = 8192
PAGE_SIZE = 128

H_A = 4
DK_A = 128
DV_A = 128
CONV_A = 4
CHUNK_A = 64
H_B = 4
DH_B = 64
DV_B = 2 * DH_B
ROPE_THETA = 10000.0
Q_BLOCK = 128
D_CONV = 512
CONV_C = 31
D_FF = 2816
EPS = 1e-6
N_BRANCH = 3
BRANCH_W = 512

QK_A = H_A * DK_A
V_A = H_A * DV_A
QKV_A = 2 * QK_A + V_A
QK_B = H_B * 2 * DH_B
V_B = H_B * DV_B
IN_SIZES = (QKV_A, V_A, H_A, H_A, QK_B, QK_B, V_B, 2 * D_CONV, N_BRANCH * D_MODEL)
N_IN = sum(IN_SIZES)

kernel_name = 'hybrid_gdn_diffattn_conformer_step'


def rmsnorm(x, g):
    xf = x.astype(jnp.float32)
    y = xf * lax.rsqrt(jnp.mean(xf * xf, axis=-1, keepdims=True) + EPS)
    return (y * g.astype(jnp.float32)).astype(x.dtype)


def layernorm(x, g, b):
    xf = x.astype(jnp.float32)
    xc = xf - jnp.mean(xf, axis=-1, keepdims=True)
    y = xc * lax.rsqrt(jnp.mean(xc * xc, axis=-1, keepdims=True) + EPS)
    return (y * g.astype(jnp.float32) + b.astype(jnp.float32)).astype(x.dtype)


def l2norm(x):
    xf = x.astype(jnp.float32)
    return (xf * lax.rsqrt(jnp.sum(xf * xf, axis=-1, keepdims=True) + EPS)).astype(x.dtype)


def swiglu(x, w_in, w_out):
    gate, up = jnp.split(x @ w_in, 2, axis=-1)
    return (jax.nn.silu(gate) * up) @ w_out


def causal_dwconv(x_ext, w):
    return lax.conv_general_dilated(x_ext, w[:, None, :].astype(x_ext.dtype), window_strides=(1,),
                                    padding='VALID', dimension_numbers=('NWC', 'WIO', 'NWC'),
                                    feature_group_count=x_ext.shape[-1])


def rope(x, pos):
    half = x.shape[-1] // 2
    inv = ROPE_THETA ** (-jnp.arange(half, dtype=jnp.float32) / half)
    ang = pos.astype(jnp.float32)[:, None] * inv[None, :]
    shape = (1, pos.shape[0]) + (1,) * (x.ndim - 3) + (half,)
    cos = jnp.cos(ang).reshape(shape)
    sin = jnp.sin(ang).reshape(shape)
    xf = x.astype(jnp.float32)
    x1, x2 = xf[..., :half], xf[..., half:]
    return jnp.concatenate([x1 * cos - x2 * sin, x1 * sin + x2 * cos], axis=-1).astype(x.dtype)


def gated_delta_rule(q, k, v, beta, g, s0):
    B, T, H, DK = k.shape
    DV = v.shape[-1]
    C = CHUNK_A
    n = -(-T // C)
    pad = n * C - T

    def prep(a):
        a = jnp.pad(a.astype(jnp.float32), [(0, 0), (0, pad)] + [(0, 0)] * (a.ndim - 2))
        return jnp.moveaxis(a.reshape((B, n, C) + a.shape[2:]), 3, 1)

    qf, kf, vf, bf, gf = prep(q), prep(k), prep(v), prep(beta), prep(g)
    gc = jnp.cumsum(gf, axis=-1)
    idx = jnp.arange(C)
    causal = idx[:, None] >= idx[None, :]
    strict = idx[:, None] > idx[None, :]
    decay = jnp.exp(jnp.where(causal, gc[..., :, None] - gc[..., None, :], -jnp.inf))
    kb = kf * bf[..., None]
    a_mat = jnp.where(strict, jnp.einsum('bhncd,bhnmd->bhncm', kb, kf) * decay, 0.0)
    eye = jnp.eye(C, dtype=jnp.float32)
    rhs = jnp.concatenate([vf * bf[..., None], kb * jnp.exp(gc)[..., None]], axis=-1)
    sol = lax.linalg.triangular_solve(eye + a_mat, rhs, left_side=True, lower=True,
                                      unit_diagonal=True)
    u = sol[..., :DV]
    w = sol[..., DV:]
    attn = jnp.where(causal, jnp.einsum('bhncd,bhnmd->bhncm', qf, kf) * decay, 0.0)

    def step(S, inp):
        q_c, k_c, u_c, w_c, attn_c, g_c = inp
        v_new = u_c - jnp.einsum('bhcd,bhdv->bhcv', w_c, S)
        o_c = (jnp.einsum('bhcd,bhdv->bhcv', q_c * jnp.exp(g_c)[..., None], S)
               + jnp.einsum('bhcm,bhmv->bhcv', attn_c, v_new))
        g_last = g_c[..., -1]
        S = (S * jnp.exp(g_last)[..., None, None]
             + jnp.einsum('bhcd,bhcv->bhdv', k_c * jnp.exp(g_last[..., None] - g_c)[..., None], v_new))
        return S, o_c

    xs = tuple(jnp.moveaxis(a, 2, 0) for a in (qf, kf, u, w, attn, gc))
    S, o = lax.scan(step, s0.astype(jnp.float32), xs)
    o = jnp.moveaxis(jnp.moveaxis(o, 0, 2), 1, 3).reshape(B, n * C, H, DV)[:, :T]
    return o.astype(v.dtype), S.astype(s0.dtype)


def diff_attend(q, k, v, q_pos, k_pos, lam):
    s = jnp.einsum('bqhcd,bkhcd->bhcqk', q, k).astype(jnp.float32)
    mask = k_pos[None, :] <= q_pos[:, None]
    p = jax.nn.softmax(jnp.where(mask, s, -jnp.inf), axis=-1)
    a = p[:, :, 0] - lam * p[:, :, 1]
    return jnp.einsum('bhqk,bkhv->bqhv', a.astype(v.dtype), v)


def block(x, pos, past_k, past_v, gdn_s0, gdn_buf, conf_buf, lw, lam_init):
    (n_f1, wf1i, wf1o, n_mix, w_in_l, gconv_w, a_log, dt_bias, g_norm, lam_vec, d_norm,
     dw_w, dw_b, ln_g, ln_b, w_br, w_o, n_f2, wf2i, wf2o) = lw
    Bn, T, _ = x.shape
    x = x + 0.5 * swiglu(rmsnorm(x, n_f1), wf1i, wf1o)
    h = rmsnorm(x, n_mix)
    splits = np.cumsum(IN_SIZES)[:-1].tolist()
    p_qkv, p_z, p_b, p_a, p_qb, p_kb, p_vb, p_glu, p_gate = jnp.split(h @ w_in_l, splits, axis=-1)

    ext_a = jnp.concatenate([gdn_buf.astype(p_qkv.dtype), p_qkv], axis=1)
    qkv = jax.nn.silu(causal_dwconv(ext_a, gconv_w))
    gdn_buf_new = ext_a[:, -(CONV_A - 1):]
    qa, ka, va = jnp.split(qkv, [QK_A, 2 * QK_A], axis=-1)
    qa = l2norm(qa.reshape(Bn, T, H_A, DK_A)) * (DK_A ** -0.5)
    ka = l2norm(ka.reshape(Bn, T, H_A, DK_A))
    va = va.reshape(Bn, T, H_A, DV_A)
    beta = jax.nn.sigmoid(p_b.astype(jnp.float32))
    g = -jnp.exp(a_log.astype(jnp.float32)) * jax.nn.softplus(p_a.astype(jnp.float32) + dt_bias.astype(jnp.float32))
    o_a, s_new = gated_delta_rule(qa, ka, va, beta, g, gdn_s0)
    o_a = rmsnorm(o_a, g_norm) * jax.nn.silu(p_z.reshape(Bn, T, H_A, DV_A))
    o_a = o_a.reshape(Bn, T, V_A)

    qb = rope(p_qb.reshape(Bn, T, H_B, 2, DH_B), pos) * (DH_B ** -0.5)
    kb = rope(p_kb.reshape(Bn, T, H_B, 2, DH_B), pos)
    vb = p_vb.reshape(Bn, T, H_B, DV_B)
    lv = lam_vec.astype(jnp.float32)
    lam = jnp.exp(jnp.sum(lv[0] * lv[1])) - jnp.exp(jnp.sum(lv[2] * lv[3])) + lam_init
    if past_k is None:
        nb = T // Q_BLOCK
        q_blocks = jnp.moveaxis(qb.reshape(Bn, nb, Q_BLOCK, H_B, 2, DH_B), 1, 0)
        pos_blocks = pos.reshape(nb, Q_BLOCK)
        o_b = lax.map(lambda qp: diff_attend(qp[0], kb, vb, qp[1], pos, lam), (q_blocks, pos_blocks))
        o_b = jnp.moveaxis(o_b, 0, 1).reshape(Bn, T, H_B, DV_B)
    else:
        n_past = past_k.shape[1]
        k_all = jnp.concatenate([past_k.reshape(Bn, n_past, H_B, 2, DH_B).astype(kb.dtype), kb], axis=1)
        v_all = jnp.concatenate([past_v.astype(vb.dtype), vb], axis=1)
        k_pos = jnp.concatenate([jnp.arange(n_past), pos])
        o_b = diff_attend(qb, k_all, v_all, pos, k_pos, lam)
    o_b = (rmsnorm(o_b, d_norm) * (1.0 - lam_init)).reshape(Bn, T, V_B)

    glu_a, glu_b = jnp.split(p_glu, 2, axis=-1)
    u = glu_a * jax.nn.sigmoid(glu_b)
    ext_c = jnp.concatenate([conf_buf.astype(u.dtype), u], axis=1)
    conf_buf_new = ext_c[:, -(CONV_C - 1):]
    o_c = jax.nn.silu(layernorm(causal_dwconv(ext_c, dw_w) + dw_b, ln_g, ln_b))

    branches = jnp.stack([o_a, o_b, o_c], axis=2)
    gates = jax.nn.sigmoid(p_gate.reshape(Bn, T, N_BRANCH, D_MODEL))
    proj = jnp.einsum('btnc,ncd->btnd', branches, w_br)
    x = x + jnp.sum(gates * proj, axis=2) @ w_o
    x = x + 0.5 * swiglu(rmsnorm(x, n_f2), wf2i, wf2o)
    return (x, kb.reshape(Bn, T, H_B, 2 * DH_B), vb, s_new, gdn_buf_new, conf_buf_new)


def setup_inputs(seed: int = 0) -> dict:
    key = jax.random.key(seed)
    keys = iter(jax.random.split(key, 48))

    def nrm(shape, scale):
        return jax.random.normal(next(keys), shape, jnp.float32) * scale

    n_pages = PAST_LEN // PAGE_SIZE
    n_used = DEC_BATCH * n_pages
    n_pool = n_used + max(n_used // 4, 1)
    perm = jax.random.permutation(next(keys), n_pool)
    page_table = perm[:n_used].reshape(DEC_BATCH, n_pages).astype(jnp.int32)
    a_log = jnp.log(jax.random.uniform(next(keys), (DEPTH, H_A), jnp.float32, 1.0, 16.0))
    dt = jnp.exp(jax.random.uniform(next(keys), (DEPTH, H_A), jnp.float32,
                                    math.log(1e-3), math.log(1e-1)))
    dt_bias = dt + jnp.log(-jnp.expm1(-dt))
    return {
        'x_prompt': nrm((BATCH, SEQ, D_MODEL), 1.0),
        'x_sample': nrm((DEC_BATCH, DEC_SEQ, D_MODEL), 1.0),
        'cache_k': nrm((DEPTH, n_pool, PAGE_SIZE, H_B, 2 * DH_B), 1.0),
        'cache_v': nrm((DEPTH, n_pool, PAGE_SIZE, H_B, DV_B), 1.0),
        'page_table': page_table,
        'state_gdn': nrm((DEPTH, DEC_BATCH, H_A, DK_A, DV_A), 0.1),
        'state_gdn_conv': nrm((DEPTH, DEC_BATCH, CONV_A - 1, QKV_A), 1.0),
        'state_conv': nrm((DEPTH, DEC_BATCH, CONV_C - 1, D_CONV), 0.5),
        'norm_ffn1': 1.0 + nrm((DEPTH, D_MODEL), 0.01),
        'w_ffn1_in': nrm((DEPTH, D_MODEL, 2 * D_FF), D_MODEL ** -0.5),
        'w_ffn1_out': nrm((DEPTH, D_FF, D_MODEL), D_FF ** -0.5),
        'norm_mix': 1.0 + nrm((DEPTH, D_MODEL), 0.01),
        'w_in': nrm((DEPTH, D_MODEL, N_IN), D_MODEL ** -0.5),
        'gdn_conv_w': nrm((DEPTH, CONV_A, QKV_A), CONV_A ** -0.5),
        'gdn_a_log': a_log,
        'gdn_dt_bias': dt_bias,
        'gdn_norm': 1.0 + nrm((DEPTH, DV_A), 0.01),
        'diff_lambda': nrm((DEPTH, 4, DH_B), 0.1),
        'diff_norm': 1.0 + nrm((DEPTH, DV_B), 0.01),
        'conv_dw_w': nrm((DEPTH, CONV_C, D_CONV), CONV_C ** -0.5),
        'conv_dw_b': nrm((DEPTH, D_CONV), 0.01),
        'conv_ln_g': 1.0 + nrm((DEPTH, D_CONV), 0.01),
        'conv_ln_b': nrm((DEPTH, D_CONV), 0.01),
        'w_branch': nrm((DEPTH, N_BRANCH, BRANCH_W, D_MODEL), BRANCH_W ** -0.5),
        'w_out': nrm((DEPTH, D_MODEL, D_MODEL), D_MODEL ** -0.5),
        'norm_ffn2': 1.0 + nrm((DEPTH, D_MODEL), 0.01),
        'w_ffn2_in': nrm((DEPTH, D_MODEL, 2 * D_FF), D_MODEL ** -0.5),
        'w_ffn2_out': nrm((DEPTH, D_FF, D_MODEL), D_FF ** -0.5),
        'norm_final': 1.0 + nrm((D_MODEL,), 0.01),
    }


def reference(x_prompt, x_sample, cache_k, cache_v, page_table, state_gdn, state_gdn_conv, state_conv,
              norm_ffn1, w_ffn1_in, w_ffn1_out, norm_mix, w_in, gdn_conv_w, gdn_a_log, gdn_dt_bias,
              gdn_norm, diff_lambda, diff_norm, conv_dw_w, conv_dw_b, conv_ln_g, conv_ln_b,
              w_branch, w_out, norm_ffn2, w_ffn2_in, w_ffn2_out, norm_final):
    bp, tp, _ = x_prompt.shape
    bs, ts, _ = x_sample.shape
    pos_p = jnp.arange(tp)
    pos_s = PAST_LEN + jnp.arange(ts)
    xp, xs = x_prompt, x_sample
    kp_l, vp_l, ks_l, vs_l, sp_l, ss_l, gcp_l, gcs_l, cp_l, cs_l = ([] for _ in range(10))
    for l in range(DEPTH):
        lam_init = 0.8 - 0.6 * math.exp(-0.3 * l)
        lw = (norm_ffn1[l], w_ffn1_in[l], w_ffn1_out[l], norm_mix[l], w_in[l], gdn_conv_w[l],
              gdn_a_log[l], gdn_dt_bias[l], gdn_norm[l], diff_lambda[l], diff_norm[l],
              conv_dw_w[l], conv_dw_b[l], conv_ln_g[l], conv_ln_b[l], w_branch[l], w_out[l],
              norm_ffn2[l], w_ffn2_in[l], w_ffn2_out[l])
        xp, k_p, v_p, s_p, gc_p, c_p = block(
            xp, pos_p, None, None,
            jnp.zeros((bp, H_A, DK_A, DV_A), state_gdn.dtype),
            jnp.zeros((bp, CONV_A - 1, QKV_A), xp.dtype),
            jnp.zeros((bp, CONV_C - 1, D_CONV), xp.dtype), lw, lam_init)
        past_k = cache_k[l, page_table].reshape(bs, PAST_LEN, H_B, 2 * DH_B)
        past_v = cache_v[l, page_table].reshape(bs, PAST_LEN, H_B, DV_B)
        xs, k_s, v_s, s_s, gc_s, c_s = block(
            xs, pos_s, past_k, past_v, state_gdn[l], state_gdn_conv[l], state_conv[l], lw, lam_init)
        kp_l.append(k_p); vp_l.append(v_p); ks_l.append(k_s); vs_l.append(v_s)
        sp_l.append(s_p); ss_l.append(s_s); gcp_l.append(gc_p); gcs_l.append(gc_s)
        cp_l.append(c_p); cs_l.append(c_s)
    y_prompt = rmsnorm(xp, norm_final)
    y_sample = rmsnorm(xs, norm_final)
    return (y_prompt, y_sample,
            jnp.stack(kp_l), jnp.stack(vp_l), jnp.stack(ks_l), jnp.stack(vs_l),
            jnp.stack(sp_l), jnp.stack(ss_l), jnp.stack(gcp_l), jnp.stack(gcs_l),
            jnp.stack(cp_l), jnp.stack(cs_l))
```

```python
import functools
import math

import jax
import jax.numpy as jnp
import numpy as np
from jax import lax
from jax.experimental import pallas as pl
from jax.experimental.pallas import tpu as pltpu

F32 = jnp.float32
BF16 = jnp.bfloat16
HIGHEST = lax.Precision.HIGHEST

D_MODEL = 1024
N_HEAD = 4
HEAD_W = 128
HALF_W = 64
BRANCH_W = N_HEAD * HEAD_W
CONV_A = 4
CHUNK = 64
CONV_C = 31
D_FF = 2816
EPS = 1e-6
ROPE_THETA = 10000.0
PAGE = 128
NEG_INF = float("-inf")

VMEM_LIMIT = 56 * 1024 * 1024


def _cparams(sem, vmem=None):
    return pltpu.CompilerParams(dimension_semantics=sem, vmem_limit_bytes=vmem)


def _rms(x, g):
    return x * lax.rsqrt(jnp.mean(x * x, axis=-1, keepdims=True) + EPS) * g


def _sigmoid(x):
    return 1.0 / (1.0 + jnp.exp(-x))


def _silu(x):
    return x * _sigmoid(x)


def _softplus(x):
    return jnp.maximum(x, 0.0) + jnp.log1p(jnp.exp(-jnp.abs(x)))


def _row_tile(m, pref):
    return pref if m % pref == 0 else m


def _ffn_kernel(x_ref, g_ref, wg_ref, wu_ref, wo_ref, o_ref, h_ref, acc_ref):
    j = pl.program_id(1)

    @pl.when(j == 0)
    def _():
        h_ref[...] = _rms(x_ref[...], g_ref[...]).astype(BF16)

    h = h_ref[...]
    gate = jnp.dot(h, wg_ref[...], preferred_element_type=F32)
    up = jnp.dot(h, wu_ref[...], preferred_element_type=F32)
    act = (_silu(gate) * up).astype(BF16)
    contrib = jnp.dot(act, wo_ref[...], preferred_element_type=F32)

    @pl.when(j == 0)
    def _():
        acc_ref[...] = contrib

    @pl.when(j > 0)
    def _():
        acc_ref[...] += contrib

    @pl.when(j == pl.num_programs(1) - 1)
    def _():
        o_ref[...] = x_ref[...] + 0.5 * acc_ref[...]


def _ffn(x, g, w_in, w_out, tm_pref=512, n_chunk=2):
    m, d = x.shape
    tm = _row_tile(m, tm_pref)
    tf = D_FF // n_chunk
    return pl.pallas_call(
        _ffn_kernel,
        grid=(m // tm, n_chunk),
        in_specs=[
            pl.BlockSpec((tm, d), lambda i, j: (i, 0)),
            pl.BlockSpec((1, d), lambda i, j: (0, 0)),
            pl.BlockSpec((d, tf), lambda i, j: (0, j)),
            pl.BlockSpec((d, tf), lambda i, j: (0, n_chunk + j)),
            pl.BlockSpec((tf, d), lambda i, j: (j, 0)),
        ],
        out_specs=pl.BlockSpec((tm, d), lambda i, j: (i, 0)),
        out_shape=jax.ShapeDtypeStruct((m, d), F32),
        scratch_shapes=[pltpu.VMEM((tm, d), BF16), pltpu.VMEM((tm, d), F32)],
        compiler_params=_cparams(("parallel", "arbitrary"), VMEM_LIMIT),
        name="ffn",
    )(x, g.reshape(1, d), w_in, w_in, w_out)


def _proj_gdn_kernel(x_ref, g_ref, wqkv_ref, wz_ref, wba_ref, qkv_ref, z_ref, ba_ref):
    h = _rms(x_ref[...], g_ref[...]).astype(BF16)
    qkv_ref[...] = jnp.dot(h, wqkv_ref[...], preferred_element_type=F32)
    z_ref[...] = jnp.dot(h, wz_ref[...], preferred_element_type=F32)
    ba_ref[...] = jnp.dot(h, wba_ref[...], preferred_element_type=F32)


def _proj_gdn(x, g, wqkv, wz, wba, tm_pref=512):
    m, d = x.shape
    tm = _row_tile(m, tm_pref)
    row = lambda w: pl.BlockSpec((tm, w), lambda i: (i, 0))
    full = lambda a: pl.BlockSpec(a.shape, lambda i: (0, 0))
    return pl.pallas_call(
        _proj_gdn_kernel,
        grid=(m // tm,),
        in_specs=[row(d), full(g), full(wqkv), full(wz), full(wba)],
        out_specs=[row(wqkv.shape[1]), row(wz.shape[1]), row(wba.shape[1])],
        out_shape=[jax.ShapeDtypeStruct((m, w.shape[1]), F32) for w in (wqkv, wz, wba)],
        compiler_params=_cparams(("parallel",), VMEM_LIMIT),
        name="proj_gdn",
    )(x, g, wqkv, wz, wba)


def _rope_table_kernel(inv_ref, cos_ref, sin_ref, *, pos0, tm):
    i = pl.program_id(0)
    pos = lax.broadcasted_iota(jnp.int32, (tm, HEAD_W), 0) + (i * tm + pos0)
    ang = pos.astype(F32) * inv_ref[...]
    lane = lax.broadcasted_iota(jnp.int32, (tm, HEAD_W), 1)
    first = (lane % HALF_W) < (HALF_W // 2)
    s = jnp.sin(ang)
    cos_ref[...] = jnp.cos(ang)
    sin_ref[...] = jnp.where(first, -s, s)


def _rope_table(pos0, n):
    half = HALF_W // 2
    inv = ROPE_THETA ** (-jnp.arange(half, dtype=F32) / half)
    inv_lane = jnp.tile(inv, HEAD_W // half).reshape(1, HEAD_W)
    tm = _row_tile(n, 512)
    return pl.pallas_call(
        functools.partial(_rope_table_kernel, pos0=pos0, tm=tm),
        grid=(n // tm,),
        in_specs=[pl.BlockSpec((1, HEAD_W), lambda i: (0, 0))],
        out_specs=[pl.BlockSpec((tm, HEAD_W), lambda i: (i, 0))] * 2,
        out_shape=[jax.ShapeDtypeStruct((n, HEAD_W), F32)] * 2,
        compiler_params=_cparams(("parallel",)),
        name="rope_table",
    )(inv_lane)


def _proj_attn_kernel(x_ref, g_ref, wq_ref, wk_ref, wv_ref, cos_ref, sin_ref, q_ref, k_ref, v_ref):
    h = _rms(x_ref[...], g_ref[...]).astype(BF16)
    cos = cos_ref[...]
    sin = sin_ref[...]
    lane = lax.broadcasted_iota(jnp.int32, cos.shape, 1)
    first = (lane % HALF_W) < (HALF_W // 2)

    def rope(p, scale, out_ref):
        for hd in range(N_HEAD):
            xh = p[:, hd * HEAD_W:(hd + 1) * HEAD_W]
            partner = jnp.where(first, pltpu.roll(xh, HEAD_W - HALF_W // 2, 1), pltpu.roll(xh, HALF_W // 2, 1))
            out_ref[:, hd * HEAD_W:(hd + 1) * HEAD_W] = ((xh * cos + partner * sin) * scale).astype(out_ref.dtype)

    rope(jnp.dot(h, wq_ref[...], preferred_element_type=F32), HALF_W ** -0.5, q_ref)
    rope(jnp.dot(h, wk_ref[...], preferred_element_type=F32), 1.0, k_ref)
    v_ref[...] = jnp.dot(h, wv_ref[...], preferred_element_type=F32)


def _proj_attn(x, g, wq, wk, wv, cos, sin, rows_per_seq, q_dtype, tm_pref=512):
    m, d = x.shape
    tm = _row_tile(rows_per_seq, tm_pref)
    n_t = rows_per_seq // tm
    row = lambda w: pl.BlockSpec((tm, w), lambda i: (i, 0))
    full = lambda a: pl.BlockSpec(a.shape, lambda i: (0, 0))
    tab = pl.BlockSpec((tm, HEAD_W), lambda i: (i % n_t, 0))
    return pl.pallas_call(
        _proj_attn_kernel,
        grid=(m // tm,),
        in_specs=[row(d), full(g), full(wq), full(wk), full(wv), tab, tab],
        out_specs=[row(BRANCH_W)] * 3,
        out_shape=[jax.ShapeDtypeStruct((m, BRANCH_W), q_dtype),
                   jax.ShapeDtypeStruct((m, BRANCH_W), F32),
                   jax.ShapeDtypeStruct((m, BRANCH_W), F32)],
        compiler_params=_cparams(("parallel",), VMEM_LIMIT),
        name="proj_attn",
    )(x, g, wq, wk, wv, cos, sin)


def _proj_cg_kernel(x_ref, g_ref, wa_ref, wb_ref, wgate_ref, u_ref, gates_ref):
    h = _rms(x_ref[...], g_ref[...]).astype(BF16)
    a = jnp.dot(h, wa_ref[...], preferred_element_type=F32)
    b = jnp.dot(h, wb_ref[...], preferred_element_type=F32)
    u_ref[...] = a * _sigmoid(b)
    gates_ref[...] = _sigmoid(jnp.dot(h, wgate_ref[...], preferred_element_type=F32))


def _proj_cg(x, g, wa, wb, wgate, tm_pref=512):
    m, d = x.shape
    tm = _row_tile(m, tm_pref)
    row = lambda w: pl.BlockSpec((tm, w), lambda i: (i, 0))
    full = lambda a: pl.BlockSpec(a.shape, lambda i: (0, 0))
    return pl.pallas_call(
        _proj_cg_kernel,
        grid=(m // tm,),
        in_specs=[row(d), full(g), full(wa), full(wb), full(wgate)],
        out_specs=[row(wa.shape[1]), row(wgate.shape[1])],
        out_shape=[jax.ShapeDtypeStruct((m, wa.shape[1]), F32),
                   jax.ShapeDtypeStruct((m, wgate.shape[1]), F32)],
        compiler_params=_cparams(("parallel",), VMEM_LIMIT),
        name="proj_cg",
    )(x, g, wa, wb, wgate)


def _l2norm(x):
    return x * lax.rsqrt(jnp.sum(x * x, axis=-1, keepdims=True) + EPS)


def _gdn_gates(ba, arow, dtrow):
    beta = _sigmoid(ba)
    g = -jnp.exp(arow) * _softplus(ba + dtrow)
    return beta, g


def _gdn_kernel(qkv_ref, z_ref, ba_ref, cw_ref, arow_ref, dtrow_ref, gn_ref,
                o_ref, s_out_ref, ext_ref, s_ref, *, tm):
    t = pl.program_id(1)
    nc = tm // CHUNK
    qkv_w = N_HEAD * HEAD_W

    @pl.when(t == 0)
    def _():
        ext_ref[0:8, :] = jnp.zeros((8, ext_ref.shape[1]), F32)
        s_ref[...] = jnp.zeros(s_ref.shape, F32)

    ext_ref[8:8 + tm, :] = qkv_ref[...]
    y = cw_ref[CONV_A - 1:CONV_A, :] * ext_ref[8:8 + tm, :]
    for j in range(CONV_A - 1):
        off = 8 - (CONV_A - 1) + j
        y = y + cw_ref[j:j + 1, :] * ext_ref[off:off + tm, :]
    ext_ref[0:8, :] = ext_ref[tm:tm + 8, :]
    qkv = _silu(y)

    beta, g_all = _gdn_gates(ba_ref[...], arow_ref[...], dtrow_ref[...])
    r = lax.broadcasted_iota(jnp.int32, (tm, tm), 0)
    c = lax.broadcasted_iota(jnp.int32, (tm, tm), 1)
    lbd = jnp.where(r // CHUNK == c // CHUNK, jnp.where(c <= r, 1.0, 0.0), 0.0)
    gc = jnp.dot(lbd, g_all, precision=HIGHEST, preferred_element_type=F32)
    gc_t = gc.T

    ci = lax.broadcasted_iota(jnp.int32, (CHUNK, CHUNK), 0)
    cj = lax.broadcasted_iota(jnp.int32, (CHUNK, CHUNK), 1)
    eye = jnp.where(ci == cj, 1.0, 0.0)

    per = []
    n_list, rhs_list = [], []
    for hd in range(N_HEAD):
        qh = _l2norm(qkv[:, hd * HEAD_W:(hd + 1) * HEAD_W]) * (HEAD_W ** -0.5)
        kh = _l2norm(qkv[:, qkv_w + hd * HEAD_W:qkv_w + (hd + 1) * HEAD_W])
        vh = qkv[:, 2 * qkv_w + hd * HEAD_W:2 * qkv_w + (hd + 1) * HEAD_W]
        gcol = gc[:, N_HEAD + hd:N_HEAD + hd + 1]
        bcol = beta[:, hd:hd + 1]
        egc = jnp.exp(gcol)
        qb16 = qh.astype(BF16)
        kb16 = kh.astype(BF16)
        for ch in range(nc):
            sl = slice(ch * CHUNK, (ch + 1) * CHUNK)
            grow = gc_t[N_HEAD + hd:N_HEAD + hd + 1, sl]
            diff = gcol[sl] - grow
            decay = jnp.exp(jnp.where(ci >= cj, diff, NEG_INF))
            kk = lax.dot_general(kb16[sl], kb16[sl], (((1,), (1,)), ((), ())), preferred_element_type=F32)
            qk = lax.dot_general(qb16[sl], kb16[sl], (((1,), (1,)), ((), ())), preferred_element_type=F32)
            a_mat = jnp.where(ci > cj, kk * decay * bcol[sl], 0.0)
            attn = qk * decay
            n_list.append(-a_mat)
            rhs_list.append(jnp.concatenate([vh[sl] * bcol[sl], kh[sl] * (bcol[sl] * egc[sl])], axis=1))
            g_last = gcol[ch * CHUNK + CHUNK - 1:(ch + 1) * CHUNK]
            per.append((hd, ch, qh[sl] * egc[sl], kh[sl] * jnp.exp(g_last - gcol[sl]), attn, jnp.exp(g_last)))

    x = jnp.stack(n_list)
    tinv = eye[None] + x
    for _ in range(5):
        x = jnp.einsum("bij,bjk->bik", x, x, precision=HIGHEST, preferred_element_type=F32)
        tinv = tinv + jnp.einsum("bij,bjk->bik", tinv, x, precision=HIGHEST, preferred_element_type=F32)
    sol = jnp.einsum("bij,bjk->bik", tinv, jnp.stack(rhs_list), precision=HIGHEST, preferred_element_type=F32)

    gn = gn_ref[...]
    for hd in range(N_HEAD):
        s = s_ref[hd]
        for ch in range(nc):
            b = hd * nc + ch
            _, _, qg, kd, attn, eg_last = per[b]
            u = sol[b, :, :HEAD_W]
            w = sol[b, :, HEAD_W:]
            s16 = s.astype(BF16)
            v_new = u - jnp.dot(w.astype(BF16), s16, preferred_element_type=F32)
            o = (jnp.dot(qg.astype(BF16), s16, preferred_element_type=F32)
                 + jnp.dot(attn.astype(BF16), v_new.astype(BF16), preferred_element_type=F32))
            s = s * eg_last + lax.dot_general(kd.astype(BF16), v_new.astype(BF16), (((0,), (0,)), ((), ())),
                                              preferred_element_type=F32)
            sl = slice(ch * CHUNK, (ch + 1) * CHUNK)
            zc = z_ref[sl, hd * HEAD_W:(hd + 1) * HEAD_W]
            o_ref[sl, hd * HEAD_W:(hd + 1) * HEAD_W] = (_rms(o, gn) * _silu(zc)).astype(o_ref.dtype)
        s_ref[hd] = s

    @pl.when(t == pl.num_programs(1) - 1)
    def _():
        s_out_ref[0] = s_ref[...]


def _gdn(qkv, z, ba, conv_w, arow, dtrow, gnorm, n_seq, tm_pref=256):
    m, w = qkv.shape
    t_len = m // n_seq
    tm = _row_tile(t_len, tm_pref)
    n_t = t_len // tm
    row = lambda wd: pl.BlockSpec((tm, wd), lambda b, t: (b * n_t + t, 0))
    full = lambda a: pl.BlockSpec(a.shape, lambda b, t: (0, 0))
    return pl.pallas_call(
        functools.partial(_gdn_kernel, tm=tm),
        grid=(n_seq, n_t),
        in_specs=[row(w), row(BRANCH_W), row(HEAD_W), full(conv_w), full(arow), full(dtrow), full(gnorm)],
        out_specs=[row(BRANCH_W),
                   pl.BlockSpec((1, N_HEAD, HEAD_W, HEAD_W), lambda b, t: (b, 0, 0, 0))],
        out_shape=[jax.ShapeDtypeStruct((m, BRANCH_W), BF16),
                   jax.ShapeDtypeStruct((n_seq, N_HEAD, HEAD_W, HEAD_W), F32)],
        scratch_shapes=[pltpu.VMEM((tm + 8, w), F32), pltpu.VMEM((N_HEAD, HEAD_W, HEAD_W), F32)],
        compiler_params=_cparams(("parallel", "arbitrary"), VMEM_LIMIT),
        name="gdn",
    )(qkv, z, ba, conv_w, arow, dtrow, gnorm)


def _lambda(lv, lam_init):
    s01 = jnp.sum(lv[0:1] * lv[1:2], axis=-1, keepdims=True)
    s23 = jnp.sum(lv[2:3] * lv[3:4], axis=-1, keepdims=True)
    return jnp.exp(s01) - jnp.exp(s23) + lam_init


def _attn_kernel(qi_tab, ki_tab, q_ref, k_ref, v_ref, lv_ref, dn_ref, o_ref,
                 q1_ref, q2_ref, m1, l1, a1, m2, l2, a2, *, tq, lam_init):
    step = pl.program_id(2)
    qi = qi_tab[step]
    ki = ki_tab[step]

    @pl.when(ki == 0)
    def _():
        q = q_ref[...]
        lane = lax.broadcasted_iota(jnp.int32, q.shape, 1)
        q1_ref[...] = jnp.where(lane < HALF_W, q, jnp.zeros_like(q))
        q2_ref[...] = jnp.where(lane >= HALF_W, q, jnp.zeros_like(q))
        for m, l, a in ((m1, l1, a1), (m2, l2, a2)):
            m[...] = jnp.full(m.shape, NEG_INF, F32)
            l[...] = jnp.zeros(l.shape, F32)
            a[...] = jnp.zeros(a.shape, F32)

    def update(masked):
        kb = k_ref[...].astype(BF16)
        vb = v_ref[...].astype(BF16)
        for qr, m, l, a in ((q1_ref, m1, l1, a1), (q2_ref, m2, l2, a2)):
            sc = lax.dot_general(qr[...], kb, (((1,), (1,)), ((), ())), preferred_element_type=F32)
            if masked:
                row = lax.broadcasted_iota(jnp.int32, sc.shape, 0)
                col = lax.broadcasted_iota(jnp.int32, sc.shape, 1)
                sc = jnp.where(col <= row, sc, NEG_INF)
            m_new = jnp.maximum(m[...], jnp.max(sc, axis=-1, keepdims=True))
            alpha = jnp.exp(m[...] - m_new)
            p = jnp.exp(sc - m_new)
            l[...] = alpha * l[...] + jnp.sum(p, axis=-1, keepdims=True)
            a[...] = alpha * a[...] + jnp.dot(p.astype(BF16), vb, preferred_element_type=F32)
            m[...] = m_new

    @pl.when(ki < qi)
    def _():
        update(False)

    @pl.when(ki == qi)
    def _():
        update(True)
        lam = _lambda(lv_ref[...], lam_init)
        o = a1[...] / l1[...] - lam * (a2[...] / l2[...])
        o_ref[...] = (_rms(o, dn_ref[...]) * (1.0 - lam_init)).astype(o_ref.dtype)


def _attn(q, k, v, lv, dnorm, n_seq, lam_init, tq_pref=512):
    m, _ = q.shape
    t_len = m // n_seq
    tq = _row_tile(t_len, tq_pref)
    n_q = t_len // tq
    pairs = [(qi, ki) for qi in range(n_q) for ki in range(qi + 1)]
    qi_tab = jnp.asarray(np.array([p[0] for p in pairs], np.int32))
    ki_tab = jnp.asarray(np.array([p[1] for p in pairs], np.int32))
    qmap = lambda b, h, s, qt, kt: (b * n_q + qt[s], h)
    kmap = lambda b, h, s, qt, kt: (b * n_q + kt[s], h)
    full = lambda a: pl.BlockSpec(a.shape, lambda b, h, s, qt, kt: (0, 0))
    stat = pltpu.VMEM((tq, 1), F32)
    accs = pltpu.VMEM((tq, HEAD_W), F32)
    return pl.pallas_call(
        functools.partial(_attn_kernel, tq=tq, lam_init=lam_init),
        grid_spec=pltpu.PrefetchScalarGridSpec(
            num_scalar_prefetch=2,
            grid=(n_seq, N_HEAD, len(pairs)),
            in_specs=[pl.BlockSpec((tq, HEAD_W), qmap), pl.BlockSpec((tq, HEAD_W), kmap),
                      pl.BlockSpec((tq, HEAD_W), kmap), full(lv), full(dnorm)],
            out_specs=pl.BlockSpec((tq, HEAD_W), qmap),
            scratch_shapes=[pltpu.VMEM((tq, HEAD_W), BF16), pltpu.VMEM((tq, HEAD_W), BF16),
                            stat, stat, accs, stat, stat, accs],
        ),
        out_shape=jax.ShapeDtypeStruct((m, BRANCH_W), BF16),
        compiler_params=_cparams(("parallel", "parallel", "arbitrary"), VMEM_LIMIT),
        name="diff_attn",
    )(qi_tab, ki_tab, q, k, v, lv, dnorm)


def _ln_swish(y, g, b):
    yc = y - jnp.mean(y, axis=-1, keepdims=True)
    yn = yc * lax.rsqrt(jnp.mean(yc * yc, axis=-1, keepdims=True) + EPS) * g + b
    return _silu(yn)


CARRY_C = 32


def _conf_kernel(u_ref, w_ref, b_ref, lg_ref, lb_ref, o_ref, ext_ref, *, tm):
    t = pl.program_id(1)

    @pl.when(t == 0)
    def _():
        ext_ref[0:CARRY_C, :] = jnp.zeros((CARRY_C, ext_ref.shape[1]), F32)

    ext_ref[CARRY_C:CARRY_C + tm, :] = u_ref[...]
    base = CARRY_C - (CONV_C - 1)
    acc = w_ref[0:1, :] * ext_ref[base:base + tm, :]
    for j in range(1, CONV_C):
        acc = acc + w_ref[j:j + 1, :] * ext_ref[base + j:base + j + tm, :]
    ext_ref[0:CARRY_C, :] = ext_ref[tm:tm + CARRY_C, :]
    o_ref[...] = _ln_swish(acc + b_ref[...], lg_ref[...], lb_ref[...]).astype(o_ref.dtype)


def _conf(u, w, b, lg, lb, n_seq, tm_pref=256):
    m, d = u.shape
    t_len = m // n_seq
    tm = _row_tile(t_len, tm_pref)
    n_t = t_len // tm
    row = pl.BlockSpec((tm, d), lambda s, t: (s * n_t + t, 0))
    full = lambda a: pl.BlockSpec(a.shape, lambda s, t: (0, 0))
    return pl.pallas_call(
        functools.partial(_conf_kernel, tm=tm),
        grid=(n_seq, n_t),
        in_specs=[row, full(w), full(b), full(lg), full(lb)],
        out_specs=row,
        out_shape=jax.ShapeDtypeStruct((m, d), BF16),
        scratch_shapes=[pltpu.VMEM((tm + CARRY_C, d), F32)],
        compiler_params=_cparams(("parallel", "arbitrary")),
        name="conf_conv",
    )(u, w, b, lg, lb)


def _merge_kernel(x_ref, oa_ref, ob_ref, oc_ref, gates_ref, wbr_ref, wo_ref, out_ref):
    mix = None
    for b, o in enumerate((oa_ref, ob_ref, oc_ref)):
        pr = jnp.dot(o[...], wbr_ref[b], preferred_element_type=F32)
        term = gates_ref[:, b * D_MODEL:(b + 1) * D_MODEL] * pr
        mix = term if mix is None else mix + term
    out_ref[...] = x_ref[...] + jnp.dot(mix.astype(BF16), wo_ref[...], preferred_element_type=F32)


def _merge(x, oa, ob, oc, gates, wbr, wo, tm_pref=512):
    m, d = x.shape
    tm = _row_tile(m, tm_pref)
    row = lambda w: pl.BlockSpec((tm, w), lambda i: (i, 0))
    return pl.pallas_call(
        _merge_kernel,
        grid=(m // tm,),
        in_specs=[row(d), row(BRANCH_W), row(BRANCH_W), row(BRANCH_W), row(3 * d),
                  pl.BlockSpec(wbr.shape, lambda i: (0, 0, 0)), pl.BlockSpec(wo.shape, lambda i: (0, 0))],
        out_specs=row(d),
        out_shape=jax.ShapeDtypeStruct((m, d), F32),
        compiler_params=_cparams(("parallel",), VMEM_LIMIT),
        name="merge",
    )(x, oa, ob, oc, gates, wbr, wo)


def _final_norm_kernel(x_ref, g_ref, o_ref):
    o_ref[...] = _rms(x_ref[...], g_ref[...])


def _final_norm(x, g, tm_pref=512):
    m, d = x.shape
    tm = _row_tile(m, tm_pref)
    return pl.pallas_call(
        _final_norm_kernel,
        grid=(m // tm,),
        in_specs=[pl.BlockSpec((tm, d), lambda i: (i, 0)), pl.BlockSpec((1, d), lambda i: (0, 0))],
        out_specs=pl.BlockSpec((tm, d), lambda i: (i, 0)),
        out_shape=jax.ShapeDtypeStruct((m, d), F32),
        compiler_params=_cparams(("parallel",)),
        name="final_norm",
    )(x, g.reshape(1, d))


def _gdn_dec_kernel(qkv_ref, z_ref, ba_ref, st_ref, s0_ref, cw_ref, arow_ref, dtrow_ref, gn_ref,
                    o_ref, st_out_ref, s_out_ref, *, bt):
    qkv_w = N_HEAD * HEAD_W
    x = qkv_ref[...]
    y = cw_ref[CONV_A - 1:CONV_A, :] * x
    for j in range(CONV_A - 1):
        y = y + cw_ref[j:j + 1, :] * st_ref[j]
    for j in range(CONV_A - 2):
        st_out_ref[j] = st_ref[j + 1]
    st_out_ref[CONV_A - 2] = x
    qkv = _silu(y)
    beta, g_all = _gdn_gates(ba_ref[...], arow_ref[...], dtrow_ref[...])
    gn = gn_ref[...]
    pad = jnp.zeros((HEAD_W - bt, HEAD_W), F32)
    for hd in range(N_HEAD):
        qh = _l2norm(qkv[:, hd * HEAD_W:(hd + 1) * HEAD_W]) * (HEAD_W ** -0.5)
        kh = _l2norm(qkv[:, qkv_w + hd * HEAD_W:qkv_w + (hd + 1) * HEAD_W])
        vh = qkv[:, 2 * qkv_w + hd * HEAD_W:2 * qkv_w + (hd + 1) * HEAD_W]
        bcol = beta[:, hd:hd + 1]
        eg = jnp.exp(g_all[:, N_HEAD + hd:N_HEAD + hd + 1])
        qk = jnp.sum(qh * kh, axis=-1, keepdims=True)
        w_t = jnp.concatenate([kh * (bcol * eg), pad], axis=0).T
        q_t = jnp.concatenate([qh * eg, pad], axis=0).T
        k_t = jnp.concatenate([kh, pad], axis=0).T
        for b in range(bt):
            s = s0_ref[b, hd]
            ws = jnp.sum(w_t[:, b:b + 1] * s, axis=0, keepdims=True)
            qs = jnp.sum(q_t[:, b:b + 1] * s, axis=0, keepdims=True)
            v_new = vh[b:b + 1] * bcol[b:b + 1] - ws
            o = qs + qk[b:b + 1] * v_new
            s_out_ref[b, hd] = s * eg[b:b + 1] + k_t[:, b:b + 1] * v_new
            zc = z_ref[b:b + 1, hd * HEAD_W:(hd + 1) * HEAD_W]
            o_ref[b:b + 1, hd * HEAD_W:(hd + 1) * HEAD_W] = _rms(o, gn) * _silu(zc)


def _gdn_dec(qkv, z, ba, st, s0, conv_w, arow, dtrow, gnorm, bt=8):
    n, w = qkv.shape
    row = lambda wd: pl.BlockSpec((bt, wd), lambda i: (i, 0))
    full = lambda a: pl.BlockSpec(a.shape, lambda i: (0, 0))
    st_spec = pl.BlockSpec((CONV_A - 1, bt, w), lambda i: (0, i, 0))
    s_spec = pl.BlockSpec((bt, N_HEAD, HEAD_W, HEAD_W), lambda i: (i, 0, 0, 0))
    return pl.pallas_call(
        functools.partial(_gdn_dec_kernel, bt=bt),
        grid=(n // bt,),
        in_specs=[row(w), row(BRANCH_W), row(HEAD_W), st_spec, s_spec,
                  full(conv_w), full(arow), full(dtrow), full(gnorm)],
        out_specs=[row(BRANCH_W), st_spec, s_spec],
        out_shape=[jax.ShapeDtypeStruct((n, BRANCH_W), F32),
                   jax.ShapeDtypeStruct(st.shape, F32),
                   jax.ShapeDtypeStruct(s0.shape, F32)],
        compiler_params=_cparams(("parallel",)),
        name="gdn_decode",
    )(qkv, z, ba, st, s0, conv_w, arow, dtrow, gnorm)


def _attn_dec_kernel(pt_ref, q_ref, kn_ref, vn_ref, lv_ref, dn_ref, *rest, n_page, lam_init):
    k_refs = rest[:n_page]
    v_refs = rest[n_page:2 * n_page]
    o_ref, m_ref, l_ref, a_ref = rest[2 * n_page:]
    j = pl.program_id(1)

    @pl.when(j == 0)
    def _():
        m_ref[...] = jnp.full(m_ref.shape, NEG_INF, F32)
        l_ref[...] = jnp.zeros(l_ref.shape, F32)
        a_ref[...] = jnp.zeros(a_ref.shape, F32)

    q4 = q_ref[0]
    lane = lax.broadcasted_iota(jnp.int32, q4.shape, 1)
    qmat = jnp.concatenate([jnp.where(lane < HALF_W, q4, 0.0), jnp.where(lane >= HALF_W, q4, 0.0)], axis=0)
    qmat16 = qmat.astype(BF16)
    rows = PAGE * N_HEAD
    r = lax.broadcasted_iota(jnp.int32, (2 * N_HEAD, rows), 0)
    c = lax.broadcasted_iota(jnp.int32, (2 * N_HEAD, rows), 1)
    valid = (c % N_HEAD) == (r % N_HEAD)
    for p in range(n_page):
        sc = lax.dot_general(qmat16, k_refs[p][...].astype(BF16), (((1,), (1,)), ((), ())),
                             preferred_element_type=F32)
        sc = jnp.where(valid, sc, NEG_INF)
        m_new = jnp.maximum(m_ref[...], jnp.max(sc, axis=-1, keepdims=True))
        alpha = jnp.exp(m_ref[...] - m_new)
        pr = jnp.exp(sc - m_new)
        l_ref[...] = alpha * l_ref[...] + jnp.sum(pr, axis=-1, keepdims=True)
        a_ref[...] = alpha * a_ref[...] + jnp.dot(pr.astype(BF16), v_refs[p][...].astype(BF16),
                                                  preferred_element_type=F32)
        m_ref[...] = m_new

    @pl.when(j == pl.num_programs(1) - 1)
    def _():
        k8 = jnp.concatenate([kn_ref[0], kn_ref[0]], axis=0)
        v8 = jnp.concatenate([vn_ref[0], vn_ref[0]], axis=0)
        s_self = jnp.sum(qmat * k8, axis=-1, keepdims=True)
        m_new = jnp.maximum(m_ref[...], s_self)
        alpha = jnp.exp(m_ref[...] - m_new)
        p_self = jnp.exp(s_self - m_new)
        l_fin = alpha * l_ref[...] + p_self
        a_fin = alpha * a_ref[...] + p_self * v8
        on = a_fin / l_fin
        lam = _lambda(lv_ref[...], lam_init)
        o = on[0:N_HEAD] - lam * on[N_HEAD:2 * N_HEAD]
        o_ref[0] = _rms(o, dn_ref[...]) * (1.0 - lam_init)


def _attn_dec(q, k_new, v_new, cache_k, cache_v, page_flat, layer, lv, dnorm, lam_init, n_page=4):
    n = q.shape[0]
    pages_per_seq = page_flat.shape[0] // n
    n_page = n_page if pages_per_seq % n_page == 0 else 1
    tok = pl.BlockSpec((1, N_HEAD, HEAD_W), lambda b, j, pt: (b, 0, 0))
    full = lambda a: pl.BlockSpec(a.shape, lambda b, j, pt: (0, 0))

    def page_spec(p):
        return pl.BlockSpec((None, None, PAGE * N_HEAD, HEAD_W),
                            lambda b, j, pt: (layer, pt[b * pages_per_seq + j * n_page + p], 0, 0))

    return pl.pallas_call(
        functools.partial(_attn_dec_kernel, n_page=n_page, lam_init=lam_init),
        grid_spec=pltpu.PrefetchScalarGridSpec(
            num_scalar_prefetch=1,
            grid=(n, pages_per_seq // n_page),
            in_specs=[tok, tok, tok, full(lv), full(dnorm)]
                     + [page_spec(p) for p in range(n_page)] * 2,
            out_specs=tok,
            scratch_shapes=[pltpu.VMEM((2 * N_HEAD, 1), F32), pltpu.VMEM((2 * N_HEAD, 1), F32),
                            pltpu.VMEM((2 * N_HEAD, HEAD_W), F32)],
        ),
        out_shape=jax.ShapeDtypeStruct((n, N_HEAD, HEAD_W), F32),
        compiler_params=_cparams(("parallel", "arbitrary")),
        name="diff_attn_decode",
    )(page_flat, q.reshape(n, N_HEAD, HEAD_W), k_new.reshape(n, N_HEAD, HEAD_W),
      v_new.reshape(n, N_HEAD, HEAD_W), lv, dnorm,
      *([cache_k] * n_page), *([cache_v] * n_page))


def _conf_dec_kernel(u_ref, st_ref, w_ref, b_ref, lg_ref, lb_ref, o_ref, st_out_ref):
    u = u_ref[...]
    acc = w_ref[CONV_C - 1:CONV_C, :] * u
    for j in range(CONV_C - 1):
        acc = acc + w_ref[j:j + 1, :] * st_ref[j]
    for j in range(CONV_C - 2):
        st_out_ref[j] = st_ref[j + 1]
    st_out_ref[CONV_C - 2] = u
    o_ref[...] = _ln_swish(acc + b_ref[...], lg_ref[...], lb_ref[...])


def _conf_dec(u, st, w, b, lg, lb):
    vm = pl.BlockSpec(memory_space=pltpu.VMEM)
    return pl.pallas_call(
        _conf_dec_kernel,
        in_specs=[vm] * 6,
        out_specs=[vm, vm],
        out_shape=[jax.ShapeDtypeStruct(u.shape, F32), jax.ShapeDtypeStruct(st.shape, F32)],
        name="conf_conv_decode",
    )(u, st, w, b, lg, lb)


def _layer_weights(l, p):
    qk_a = N_HEAD * HEAD_W
    sizes = (3 * qk_a, BRANCH_W, N_HEAD, N_HEAD, BRANCH_W, BRANCH_W, BRANCH_W, 2 * BRANCH_W, 3 * D_MODEL)
    offs = np.concatenate([[0], np.cumsum(sizes)])
    w_in = p["w_in"][l]
    col = lambda i: w_in[:, offs[i]:offs[i + 1]].astype(BF16)
    wba = jnp.pad(jnp.concatenate([w_in[:, offs[2]:offs[3]], w_in[:, offs[3]:offs[4]]], axis=1),
                  ((0, 0), (0, HEAD_W - 2 * N_HEAD))).astype(BF16)
    glu = col(7)
    lane_pad = lambda v: jnp.pad(v.reshape(1, N_HEAD), ((0, 0), (N_HEAD, HEAD_W - 2 * N_HEAD)))
    r2 = lambda v: v.reshape(1, -1)
    return dict(
        n1=p["norm_ffn1"][l], f1i=p["w_ffn1_in"][l].astype(BF16), f1o=p["w_ffn1_out"][l].astype(BF16),
        nm=r2(p["norm_mix"][l]),
        wqkv=col(0), wz=col(1), wba=wba, wq=col(4), wk=col(5), wv=col(6),
        wga=glu[:, :BRANCH_W], wgb=glu[:, BRANCH_W:], wgate=col(8),
        cw=p["gdn_conv_w"][l], arow=lane_pad(p["gdn_a_log"][l]), dtrow=lane_pad(p["gdn_dt_bias"][l]),
        gn=r2(p["gdn_norm"][l]), lv=p["diff_lambda"][l], dn=r2(p["diff_norm"][l]),
        dw=p["conv_dw_w"][l], db=r2(p["conv_dw_b"][l]), lg=r2(p["conv_ln_g"][l]), lb=r2(p["conv_ln_b"][l]),
        wbr=p["w_branch"][l].astype(BF16), wo=p["w_out"][l].astype(BF16),
        n2=p["norm_ffn2"][l], f2i=p["w_ffn2_in"][l].astype(BF16), f2o=p["w_ffn2_out"][l].astype(BF16),
    )


def _prompt_layer(x, w, cos, sin, n_seq, lam_init):
    t_len = x.shape[0] // n_seq
    x = _ffn(x, w["n1"], w["f1i"], w["f1o"])
    qkv, z, ba = _proj_gdn(x, w["nm"], w["wqkv"], w["wz"], w["wba"])
    q, k, v = _proj_attn(x, w["nm"], w["wq"], w["wk"], w["wv"], cos, sin, t_len, BF16)
    u, gates = _proj_cg(x, w["nm"], w["wga"], w["wgb"], w["wgate"])
    o_a, s_new = _gdn(qkv, z, ba, w["cw"], w["arow"], w["dtrow"], w["gn"], n_seq)
    o_b = _attn(q, k, v, w["lv"], w["dn"], n_seq, lam_init)
    o_c = _conf(u, w["dw"], w["db"], w["lg"], w["lb"], n_seq)
    x = _merge(x, o_a, o_b, o_c, gates, w["wbr"], w["wo"])
    x = _ffn(x, w["n2"], w["f2i"], w["f2o"])
    gconv = qkv.reshape(n_seq, t_len, -1)[:, t_len - (CONV_A - 1):]
    cconv = u.reshape(n_seq, t_len, -1)[:, t_len - (CONV_C - 1):]
    return x, k, v, s_new, gconv, cconv


def _sample_layer(x, w, cos, sin, cache_k, cache_v, page_flat, layer, s0, gst, cst, lam_init):
    n = x.shape[0]
    x = _ffn(x, w["n1"], w["f1i"], w["f1o"])
    qkv, z, ba = _proj_gdn(x, w["nm"], w["wqkv"], w["wz"], w["wba"])
    q, k, v = _proj_attn(x, w["nm"], w["wq"], w["wk"], w["wv"], cos, sin, n, F32)
    u, gates = _proj_cg(x, w["nm"], w["wga"], w["wgb"], w["wgate"])
    o_a, gst_new, s_new = _gdn_dec(qkv, z, ba, jnp.swapaxes(gst, 0, 1), s0,
                                   w["cw"], w["arow"], w["dtrow"], w["gn"])
    o_b = _attn_dec(q, k, v, cache_k, cache_v, page_flat, layer, w["lv"], w["dn"], lam_init)
    o_c, cst_new = _conf_dec(u, jnp.swapaxes(cst, 0, 1), w["dw"], w["db"], w["lg"], w["lb"])
    x = _merge(x, o_a.astype(BF16), o_b.reshape(n, BRANCH_W).astype(BF16), o_c.astype(BF16),
               gates, w["wbr"], w["wo"])
    x = _ffn(x, w["n2"], w["f2i"], w["f2o"])
    return x, k, v, s_new, jnp.swapaxes(gst_new, 0, 1), jnp.swapaxes(cst_new, 0, 1)


def kernel(x_prompt, x_sample, cache_k, cache_v, page_table, state_gdn, state_gdn_conv, state_conv,
           norm_ffn1, w_ffn1_in, w_ffn1_out, norm_mix, w_in, gdn_conv_w, gdn_a_log, gdn_dt_bias,
           gdn_norm, diff_lambda, diff_norm, conv_dw_w, conv_dw_b, conv_ln_g, conv_ln_b,
           w_branch, w_out, norm_ffn2, w_ffn2_in, w_ffn2_out, norm_final):
    params = dict(norm_ffn1=norm_ffn1, w_ffn1_in=w_ffn1_in, w_ffn1_out=w_ffn1_out, norm_mix=norm_mix,
                  w_in=w_in, gdn_conv_w=gdn_conv_w, gdn_a_log=gdn_a_log, gdn_dt_bias=gdn_dt_bias,
                  gdn_norm=gdn_norm, diff_lambda=diff_lambda, diff_norm=diff_norm, conv_dw_w=conv_dw_w,
                  conv_dw_b=conv_dw_b, conv_ln_g=conv_ln_g, conv_ln_b=conv_ln_b, w_branch=w_branch,
                  w_out=w_out, norm_ffn2=norm_ffn2, w_ffn2_in=w_ffn2_in, w_ffn2_out=w_ffn2_out)
    depth = w_in.shape[0]
    bp, tp, d = x_prompt.shape
    bs, ts, _ = x_sample.shape
    past_len = page_table.shape[1] * PAGE
    n_pool = cache_k.shape[1]

    cos_p, sin_p = _rope_table(0, tp)
    cos_s, sin_s = _rope_table(past_len, 8)
    cos_s = jnp.broadcast_to(cos_s[0:1], (bs, HEAD_W))
    sin_s = jnp.broadcast_to(sin_s[0:1], (bs, HEAD_W))
    ck = cache_k.reshape(depth, n_pool, PAGE * N_HEAD, HEAD_W)
    cv = cache_v.reshape(depth, n_pool, PAGE * N_HEAD, HEAD_W)
    page_flat = page_table.reshape(-1)

    xp = x_prompt.reshape(bp * tp, d)
    xs = x_sample.reshape(bs * ts, d)
    outs = [[] for _ in range(10)]
    for l in range(depth):
        lam_init = 0.8 - 0.6 * math.exp(-0.3 * l)
        w = _layer_weights(l, params)
        xp, k_p, v_p, s_p, gc_p, c_p = _prompt_layer(xp, w, cos_p, sin_p, bp, lam_init)
        xs, k_s, v_s, s_s, gc_s, c_s = _sample_layer(xs, w, cos_s, sin_s, ck, cv, page_flat, l,
                                                     state_gdn[l], state_gdn_conv[l], state_conv[l], lam_init)
        vals = (k_p.reshape(bp, tp, N_HEAD, HEAD_W), v_p.reshape(bp, tp, N_HEAD, HEAD_W),
                k_s.reshape(bs, ts, N_HEAD, HEAD_W), v_s.reshape(bs, ts, N_HEAD, HEAD_W),
                s_p, s_s, gc_p, gc_s, c_p, c_s)
        for lst, val in zip(outs, vals):
            lst.append(val)
    y_prompt = _final_norm(xp, norm_final).reshape(bp, tp, d)
    y_sample = _final_norm(xs, norm_final).reshape(bs, ts, d)
    return (y_prompt, y_sample) + tuple(jnp.stack(o) for o in outs)
```

```python
import functools
import math

import jax
import jax.numpy as jnp
import numpy as np
from jax import lax
from jax.experimental import pallas as pl
from jax.experimental.pallas import tpu as pltpu

F32 = jnp.float32
BF16 = jnp.bfloat16

D_MODEL = 1024
N_HEAD = 4
HEAD_W = 128
HALF_W = 64
BRANCH_W = N_HEAD * HEAD_W
CONV_A = 4
CHUNK = 64
CONV_C = 31
D_FF = 2816
EPS = 1e-6
ROPE_THETA = 10000.0
PAGE = 128
NEG_INF = float("-inf")

VMEM_LIMIT = 56 * 1024 * 1024


def _cparams(sem, vmem=None):
    return pltpu.CompilerParams(dimension_semantics=sem, vmem_limit_bytes=vmem)


def _rms(x, g):
    return x * lax.rsqrt(jnp.mean(x * x, axis=-1, keepdims=True) + EPS) * g


def _sigmoid(x):
    return 1.0 / (1.0 + jnp.exp(-x))


def _silu(x):
    return x * _sigmoid(x)


def _softplus(x):
    return jnp.maximum(x, 0.0) + jnp.log1p(jnp.exp(-jnp.abs(x)))


def _row_tile(m, pref):
    return pref if m % pref == 0 else m


def _ffn_kernel(x_ref, g_ref, wg_ref, wu_ref, wo_ref, o_ref, h_ref, acc_ref):
    j = pl.program_id(1)

    @pl.when(j == 0)
    def _():
        h_ref[...] = _rms(x_ref[...], g_ref[...]).astype(BF16)

    h = h_ref[...]
    gate = jnp.dot(h, wg_ref[...], preferred_element_type=F32)
    up = jnp.dot(h, wu_ref[...], preferred_element_type=F32)
    act = (_silu(gate) * up).astype(BF16)
    contrib = jnp.dot(act, wo_ref[...], preferred_element_type=F32)

    @pl.when(j == 0)
    def _():
        acc_ref[...] = contrib

    @pl.when(j > 0)
    def _():
        acc_ref[...] += contrib

    @pl.when(j == pl.num_programs(1) - 1)
    def _():
        o_ref[...] = x_ref[...] + 0.5 * acc_ref[...]


def _ffn(x, g, w_in, w_out, tm_pref=512, n_chunk=2):
    m, d = x.shape
    tm = _row_tile(m, tm_pref)
    tf = D_FF // n_chunk
    return pl.pallas_call(
        _ffn_kernel,
        grid=(m // tm, n_chunk),
        in_specs=[
            pl.BlockSpec((tm, d), lambda i, j: (i, 0)),
            pl.BlockSpec((1, d), lambda i, j: (0, 0)),
            pl.BlockSpec((d, tf), lambda i, j: (0, j)),
            pl.BlockSpec((d, tf), lambda i, j: (0, n_chunk + j)),
            pl.BlockSpec((tf, d), lambda i, j: (j, 0)),
        ],
        out_specs=pl.BlockSpec((tm, d), lambda i, j: (i, 0)),
        out_shape=jax.ShapeDtypeStruct((m, d), F32),
        scratch_shapes=[pltpu.VMEM((tm, d), BF16), pltpu.VMEM((tm, d), F32)],
        compiler_params=_cparams(("parallel", "arbitrary"), VMEM_LIMIT),
        name="ffn",
    )(x, g.reshape(1, d), w_in, w_in, w_out)


def _proj_gdn_kernel(x_ref, g_ref, wqkv_ref, wz_ref, wba_ref, qkv_ref, z_ref, ba_ref):
    h = _rms(x_ref[...], g_ref[...]).astype(BF16)
    qkv_ref[...] = jnp.dot(h, wqkv_ref[...], preferred_element_type=F32)
    z_ref[...] = jnp.dot(h, wz_ref[...], preferred_element_type=F32)
    ba_ref[...] = jnp.dot(h, wba_ref[...], preferred_element_type=F32)


def _proj_gdn(x, g, wqkv, wz, wba, tm_pref=512):
    m, d = x.shape
    tm = _row_tile(m, tm_pref)
    row = lambda w: pl.BlockSpec((tm, w), lambda i: (i, 0))
    full = lambda a: pl.BlockSpec(a.shape, lambda i: (0, 0))
    return pl.pallas_call(
        _proj_gdn_kernel,
        grid=(m // tm,),
        in_specs=[row(d), full(g), full(wqkv), full(wz), full(wba)],
        out_specs=[row(wqkv.shape[1]), row(wz.shape[1]), row(wba.shape[1])],
        out_shape=[jax.ShapeDtypeStruct((m, w.shape[1]), F32) for w in (wqkv, wz, wba)],
        compiler_params=_cparams(("parallel",), VMEM_LIMIT),
        name="proj_gdn",
    )(x, g, wqkv, wz, wba)


def _rope_table_kernel(inv_ref, invc_ref, cos_ref, sin_ref, cost_ref, sint_ref, *, pos0, tm):
    i = pl.program_id(0)
    half = HALF_W // 2
    pos = lax.broadcasted_iota(jnp.int32, (tm, HEAD_W), 0) + (i * tm + pos0)
    ang = pos.astype(F32) * inv_ref[...]
    first = (lax.broadcasted_iota(jnp.int32, (tm, HEAD_W), 1) % HALF_W) < half
    s = jnp.sin(ang)
    cos_ref[...] = jnp.cos(ang)
    sin_ref[...] = jnp.where(first, -s, s)
    pos_t = lax.broadcasted_iota(jnp.int32, (HEAD_W, tm), 1) + (i * tm + pos0)
    ang_t = pos_t.astype(F32) * invc_ref[...]
    first_t = (lax.broadcasted_iota(jnp.int32, (HEAD_W, tm), 0) % HALF_W) < half
    s_t = jnp.sin(ang_t)
    cost_ref[...] = jnp.cos(ang_t)
    sint_ref[...] = jnp.where(first_t, -s_t, s_t)


def _rope_table(pos0, n):
    half = HALF_W // 2
    inv = ROPE_THETA ** (-jnp.arange(half, dtype=F32) / half)
    inv_rep = jnp.tile(inv, HEAD_W // half)
    tm = _row_tile(n, 512)
    return pl.pallas_call(
        functools.partial(_rope_table_kernel, pos0=pos0, tm=tm),
        grid=(n // tm,),
        in_specs=[pl.BlockSpec((1, HEAD_W), lambda i: (0, 0)), pl.BlockSpec((HEAD_W, 1), lambda i: (0, 0))],
        out_specs=[pl.BlockSpec((tm, HEAD_W), lambda i: (i, 0))] * 2
                  + [pl.BlockSpec((HEAD_W, tm), lambda i: (0, i))] * 2,
        out_shape=[jax.ShapeDtypeStruct((n, HEAD_W), F32)] * 2 + [jax.ShapeDtypeStruct((HEAD_W, n), F32)] * 2,
        compiler_params=_cparams(("parallel",)),
        name="rope_table",
    )(inv_rep.reshape(1, HEAD_W), inv_rep.reshape(HEAD_W, 1))


def _rope_rows(p, cos, sin, scale, out_refs):
    half = HALF_W // 2
    first = (lax.broadcasted_iota(jnp.int32, cos.shape, 1) % HALF_W) < half
    for hd in range(N_HEAD):
        xh = p[:, hd * HEAD_W:(hd + 1) * HEAD_W]
        partner = jnp.where(first, pltpu.roll(xh, HEAD_W - half, 1), pltpu.roll(xh, half, 1))
        y = (xh * cos + partner * sin) * scale
        for o in out_refs:
            if len(o.shape) == 3:
                o[:, hd, :] = y.astype(o.dtype)
            else:
                o[:, hd * HEAD_W:(hd + 1) * HEAD_W] = y.astype(o.dtype)


def _proj_attn_dec_kernel(x_ref, g_ref, wq_ref, wk_ref, wv_ref, cos_ref, sin_ref, q_ref, k_ref, v_ref):
    h = _rms(x_ref[...], g_ref[...]).astype(BF16)
    cos = cos_ref[...]
    sin = sin_ref[...]
    _rope_rows(jnp.dot(h, wq_ref[...], preferred_element_type=F32), cos, sin, HALF_W ** -0.5, (q_ref,))
    _rope_rows(jnp.dot(h, wk_ref[...], preferred_element_type=F32), cos, sin, 1.0, (k_ref,))
    v_ref[...] = jnp.dot(h, wv_ref[...], preferred_element_type=F32)


def _proj_attn_dec(x, g, wq, wk, wv, cos, sin):
    vm = pl.BlockSpec(memory_space=pltpu.VMEM)
    m = x.shape[0]
    return pl.pallas_call(
        _proj_attn_dec_kernel,
        in_specs=[vm] * 7,
        out_specs=[vm] * 3,
        out_shape=[jax.ShapeDtypeStruct((m, BRANCH_W), F32)] * 3,
        compiler_params=pltpu.CompilerParams(vmem_limit_bytes=VMEM_LIMIT),
        name="proj_attn_decode",
    )(x, g, wq, wk, wv, cos, sin)


Q_SCALE_LOG2 = (HALF_W ** -0.5) * math.log2(math.e)


def _proj_attn_kernel(x_ref, g_ref, wqt_ref, wk_ref, wv_ref, wvt_ref, cos_ref, sin_ref, cost_ref, sint_ref,
                      qt_ref, k_ref, k16_ref, v_ref, vt_ref):
    h = _rms(x_ref[...], g_ref[...]).astype(BF16)
    nt = (((1,), (1,)), ((), ()))
    _rope_rows(jnp.dot(h, wk_ref[...], preferred_element_type=F32), cos_ref[...], sin_ref[...], 1.0,
               (k_ref, k16_ref))
    v = jnp.dot(h, wv_ref[...], preferred_element_type=F32)
    for hd in range(N_HEAD):
        v_ref[:, hd, :] = v[:, hd * HEAD_W:(hd + 1) * HEAD_W]
    vt_ref[...] = lax.dot_general(wvt_ref[...], h, nt, preferred_element_type=F32).astype(vt_ref.dtype)
    qt = lax.dot_general(wqt_ref[...], h, nt, preferred_element_type=F32)
    cos_t = cost_ref[...]
    sin_t = sint_ref[...]
    half = HALF_W // 2
    for hd in range(N_HEAD):
        xh = qt[hd * HEAD_W:(hd + 1) * HEAD_W]
        partner = jnp.concatenate([xh[half:2 * half], xh[0:half], xh[3 * half:4 * half], xh[2 * half:3 * half]],
                                  axis=0)
        qt_ref[hd * HEAD_W:(hd + 1) * HEAD_W, :] = ((xh * cos_t + partner * sin_t) * Q_SCALE_LOG2).astype(qt_ref.dtype)


def _proj_attn(x, g, wqt, wk, wv, wvt, cos, sin, cos_t, sin_t, rows_per_seq, tm_pref=512):
    m, d = x.shape
    tm = _row_tile(rows_per_seq, tm_pref)
    n_t = rows_per_seq // tm
    row = lambda w: pl.BlockSpec((tm, w), lambda i: (i, 0))
    row3 = pl.BlockSpec((tm, N_HEAD, HEAD_W), lambda i: (i, 0, 0))
    colb = pl.BlockSpec((BRANCH_W, tm), lambda i: (0, i))
    full = lambda a: pl.BlockSpec(a.shape, lambda i: (0, 0))
    tab = pl.BlockSpec((tm, HEAD_W), lambda i: (i % n_t, 0))
    tab_t = pl.BlockSpec((HEAD_W, tm), lambda i: (0, i % n_t))
    return pl.pallas_call(
        _proj_attn_kernel,
        grid=(m // tm,),
        in_specs=[row(d), full(g), full(wqt), full(wk), full(wv), full(wvt), tab, tab, tab_t, tab_t],
        out_specs=[colb, row3, row(BRANCH_W), row3, colb],
        out_shape=[jax.ShapeDtypeStruct((BRANCH_W, m), BF16),
                   jax.ShapeDtypeStruct((m, N_HEAD, HEAD_W), F32),
                   jax.ShapeDtypeStruct((m, BRANCH_W), BF16),
                   jax.ShapeDtypeStruct((m, N_HEAD, HEAD_W), F32),
                   jax.ShapeDtypeStruct((BRANCH_W, m), BF16)],
        compiler_params=_cparams(("parallel",), VMEM_LIMIT),
        name="proj_attn",
    )(x, g, wqt, wk, wv, wvt, cos, sin, cos_t, sin_t)


def _proj_cg_kernel(x_ref, g_ref, wa_ref, wb_ref, wgate_ref, u_ref, gates_ref):
    h = _rms(x_ref[...], g_ref[...]).astype(BF16)
    a = jnp.dot(h, wa_ref[...], preferred_element_type=F32)
    b = jnp.dot(h, wb_ref[...], preferred_element_type=F32)
    u_ref[...] = a * _sigmoid(b)
    gates_ref[...] = _sigmoid(jnp.dot(h, wgate_ref[...], preferred_element_type=F32))


def _proj_cg(x, g, wa, wb, wgate, tm_pref=512):
    m, d = x.shape
    tm = _row_tile(m, tm_pref)
    row = lambda w: pl.BlockSpec((tm, w), lambda i: (i, 0))
    full = lambda a: pl.BlockSpec(a.shape, lambda i: (0, 0))
    return pl.pallas_call(
        _proj_cg_kernel,
        grid=(m // tm,),
        in_specs=[row(d), full(g), full(wa), full(wb), full(wgate)],
        out_specs=[row(wa.shape[1]), row(wgate.shape[1])],
        out_shape=[jax.ShapeDtypeStruct((m, wa.shape[1]), F32),
                   jax.ShapeDtypeStruct((m, wgate.shape[1]), F32)],
        compiler_params=_cparams(("parallel",), VMEM_LIMIT),
        name="proj_cg",
    )(x, g, wa, wb, wgate)


def _l2norm(x):
    return x * lax.rsqrt(jnp.sum(x * x, axis=-1, keepdims=True) + EPS)


def _split2(x):
    hi = x.astype(BF16)
    return hi, (x - hi.astype(F32)).astype(BF16)


def _split3(x):
    hi = x.astype(BF16)
    r1 = x - hi.astype(F32)
    mid = r1.astype(BF16)
    return hi, mid, (r1 - mid.astype(F32)).astype(BF16)


def _dot3(a, b):
    a_hi, a_lo = _split2(a)
    b_hi, b_lo = _split2(b)
    return (jnp.dot(a_hi, b_hi, preferred_element_type=F32) + jnp.dot(a_hi, b_lo, preferred_element_type=F32)
            + jnp.dot(a_lo, b_hi, preferred_element_type=F32))


def _gdn_gates(ba, arow, dtrow):
    beta = _sigmoid(ba)
    g = -jnp.exp(arow) * _softplus(ba + dtrow)
    return beta, g


def _gdn_kernel(qkv_ref, z_ref, ba_ref, cw_ref, arow_ref, dtrow_ref, gn_ref,
                o_ref, s_out_ref, ext_ref, s_ref, *, tm):
    t = pl.program_id(1)
    nc = tm // CHUNK
    qkv_w = N_HEAD * HEAD_W

    @pl.when(t == 0)
    def _():
        ext_ref[0:8, :] = jnp.zeros((8, ext_ref.shape[1]), F32)
        s_ref[...] = jnp.zeros(s_ref.shape, F32)

    ext_ref[8:8 + tm, :] = qkv_ref[...]
    y = cw_ref[CONV_A - 1:CONV_A, :] * ext_ref[8:8 + tm, :]
    for j in range(CONV_A - 1):
        off = 8 - (CONV_A - 1) + j
        y = y + cw_ref[j:j + 1, :] * ext_ref[off:off + tm, :]
    ext_ref[0:8, :] = ext_ref[tm:tm + 8, :]
    qkv = _silu(y)

    beta, g_all = _gdn_gates(ba_ref[...], arow_ref[...], dtrow_ref[...])
    r = lax.broadcasted_iota(jnp.int32, (tm, tm), 0)
    c = lax.broadcasted_iota(jnp.int32, (tm, tm), 1)
    same = r // CHUNK == c // CHUNK
    lbd = jnp.where(same, jnp.where(c <= r, 1.0, 0.0), 0.0).astype(BF16)
    gc = None
    for piece in _split3(g_all):
        term = jnp.dot(lbd, piece, preferred_element_type=F32)
        gc = term if gc is None else gc + term
    gc_t = gc.T

    ri = lax.broadcasted_iota(jnp.int32, (CHUNK, tm), 0)
    rc = lax.broadcasted_iota(jnp.int32, (CHUNK, tm), 1)
    eye_row = jnp.where(ri == rc % CHUNK, 1.0, 0.0)

    def to_bd(x_row):
        return jnp.where(same, jnp.concatenate([x_row] * nc, axis=0), 0.0)

    def from_bd(x_bd):
        acc = x_bd[0:CHUNK]
        for ch in range(1, nc):
            acc = acc + x_bd[ch * CHUNK:(ch + 1) * CHUNK]
        return acc

    gn = gn_ref[...]
    nt = (((1,), (1,)), ((), ()))
    heads = range(N_HEAD)
    qh = [_l2norm(qkv[:, hd * HEAD_W:(hd + 1) * HEAD_W]) * (HEAD_W ** -0.5) for hd in heads]
    kh = [_l2norm(qkv[:, qkv_w + hd * HEAD_W:qkv_w + (hd + 1) * HEAD_W]) for hd in heads]
    vh = [qkv[:, 2 * qkv_w + hd * HEAD_W:2 * qkv_w + (hd + 1) * HEAD_W] for hd in heads]
    gcol = [gc[:, N_HEAD + hd:N_HEAD + hd + 1] for hd in heads]
    grow = [gc_t[N_HEAD + hd:N_HEAD + hd + 1, :] for hd in heads]
    bcol = [beta[:, hd:hd + 1] for hd in heads]
    egc = [jnp.exp(g) for g in gcol]
    k16 = [k.astype(BF16) for k in kh]
    lower = jnp.logical_and(same, r >= c)
    decay = [jnp.exp(jnp.where(lower, gcol[hd] - grow[hd], NEG_INF)) for hd in heads]
    kk = [lax.dot_general(k16[hd], k16[hd], nt, preferred_element_type=F32) for hd in heads]
    qk = [lax.dot_general(qh[hd].astype(BF16), k16[hd], nt, preferred_element_type=F32) for hd in heads]
    n_bd = [jnp.where(r > c, -(kk[hd] * decay[hd] * bcol[hd]), 0.0) for hd in heads]
    attn16 = [(qk[hd] * decay[hd]).astype(BF16) for hd in heads]
    n_row = [from_bd(n) for n in n_bd]
    s_inv = [eye_row + n for n in n_row]
    x_row = [_dot3(n_row[hd], n_bd[hd]) for hd in heads]
    for _ in range(4):
        res = [_dot3(jnp.concatenate([s_inv[hd], x_row[hd]], axis=0), to_bd(x_row[hd])) for hd in heads]
        s_inv = [s_inv[hd] + res[hd][0:CHUNK] for hd in heads]
        x_row = [res[hd][CHUNK:2 * CHUNK] for hd in heads]
    s_inv = [s_inv[hd] + _dot3(s_inv[hd], to_bd(x_row[hd])) for hd in heads]
    sol = [_dot3(to_bd(s_inv[hd]), jnp.concatenate([vh[hd] * bcol[hd], kh[hd] * (bcol[hd] * egc[hd])], axis=1))
           for hd in heads]
    qg16 = [(qh[hd] * egc[hd]).astype(BF16) for hd in heads]
    w16 = [sol[hd][:, HEAD_W:].astype(BF16) for hd in heads]
    s = [s_ref[hd] for hd in heads]
    v_news = [[] for _ in heads]
    o_inters = [[] for _ in heads]
    for ch in range(nc):
        sl = slice(ch * CHUNK, (ch + 1) * CHUNK)
        for hd in heads:
            g_last = gcol[hd][ch * CHUNK + CHUNK - 1:(ch + 1) * CHUNK]
            res = jnp.dot(jnp.concatenate([w16[hd][sl], qg16[hd][sl]], axis=0), s[hd].astype(BF16),
                          preferred_element_type=F32)
            v_new = sol[hd][sl, :HEAD_W] - res[0:CHUNK]
            o_inters[hd].append(res[CHUNK:2 * CHUNK])
            v_news[hd].append(v_new)
            kd16 = (kh[hd][sl] * jnp.exp(g_last - gcol[hd][sl])).astype(BF16)
            s[hd] = s[hd] * jnp.exp(g_last) + lax.dot_general(kd16, v_new.astype(BF16), (((0,), (0,)), ((), ())),
                                                              preferred_element_type=F32)
    for hd in heads:
        s_ref[hd] = s[hd]
        o = jnp.concatenate(o_inters[hd], axis=0) + jnp.dot(
            attn16[hd], jnp.concatenate(v_news[hd], axis=0).astype(BF16), preferred_element_type=F32)
        zc = z_ref[:, hd * HEAD_W:(hd + 1) * HEAD_W]
        o_ref[:, hd * HEAD_W:(hd + 1) * HEAD_W] = (_rms(o, gn) * _silu(zc)).astype(o_ref.dtype)

    @pl.when(t == pl.num_programs(1) - 1)
    def _():
        s_out_ref[0] = s_ref[...]


def _gdn(qkv, z, ba, conv_w, arow, dtrow, gnorm, n_seq, tm_pref=256):
    m, w = qkv.shape
    t_len = m // n_seq
    tm = _row_tile(t_len, tm_pref)
    n_t = t_len // tm
    row = lambda wd: pl.BlockSpec((tm, wd), lambda b, t: (b * n_t + t, 0))
    full = lambda a: pl.BlockSpec(a.shape, lambda b, t: (0, 0))
    return pl.pallas_call(
        functools.partial(_gdn_kernel, tm=tm),
        grid=(n_seq, n_t),
        in_specs=[row(w), row(BRANCH_W), row(HEAD_W), full(conv_w), full(arow), full(dtrow), full(gnorm)],
        out_specs=[row(BRANCH_W),
                   pl.BlockSpec((1, N_HEAD, HEAD_W, HEAD_W), lambda b, t: (b, 0, 0, 0))],
        out_shape=[jax.ShapeDtypeStruct((m, BRANCH_W), BF16),
                   jax.ShapeDtypeStruct((n_seq, N_HEAD, HEAD_W, HEAD_W), F32)],
        scratch_shapes=[pltpu.VMEM((tm + 8, w), F32), pltpu.VMEM((N_HEAD, HEAD_W, HEAD_W), F32)],
        compiler_params=_cparams(("parallel", "arbitrary"), VMEM_LIMIT),
        name="gdn",
    )(qkv, z, ba, conv_w, arow, dtrow, gnorm)


def _lambda(lv, lam_init):
    s01 = jnp.sum(lv[0:1] * lv[1:2], axis=-1, keepdims=True)
    s23 = jnp.sum(lv[2:3] * lv[3:4], axis=-1, keepdims=True)
    return jnp.exp(s01) - jnp.exp(s23) + lam_init


def _attn_kernel(qt_ref, k_ref, vt_ref, lv_ref, dn_ref, o_ref,
                 q1_ref, q2_ref, m1, l1, a1, m2, l2, a2, *, tq, lam_init):
    qi = pl.program_id(2)
    qt = qt_ref[...]
    chan = lax.broadcasted_iota(jnp.int32, qt.shape, 0)
    q1_ref[...] = jnp.where(chan < HALF_W, qt, jnp.zeros_like(qt))
    q2_ref[...] = jnp.where(chan >= HALF_W, qt, jnp.zeros_like(qt))
    maps = ((q1_ref, m1, l1, a1), (q2_ref, m2, l2, a2))
    for _, m, l, a in maps:
        m[...] = jnp.full(m.shape, NEG_INF, F32)
        l[...] = jnp.zeros(l.shape, F32)
        a[...] = jnp.zeros(a.shape, F32)

    def tile(ki, masked):
        off = pl.multiple_of(ki * tq, tq)
        k16 = k_ref[pl.ds(off, tq), :]
        vt16 = vt_ref[:, pl.ds(off, tq)]
        sts = [jnp.dot(k16, qr[...], preferred_element_type=F32) for qr, _, _, _ in maps]
        olds = [(m[...], l[...], a[...]) for _, m, l, a in maps]
        for st, (m_old, l_old, a_old), (_, m, l, a) in zip(sts, olds, maps):
            if masked:
                kv = lax.broadcasted_iota(jnp.int32, st.shape, 0)
                qq = lax.broadcasted_iota(jnp.int32, st.shape, 1)
                st = jnp.where(kv <= qq, st, NEG_INF)
            m_new = jnp.maximum(m_old, jnp.max(st, axis=0, keepdims=True))
            alpha = jnp.exp2(m_old - m_new)
            p = jnp.exp2(st - m_new)
            l[...] = alpha * l_old + jnp.sum(p, axis=0, keepdims=True)
            a[...] = alpha * a_old + jnp.dot(vt16, p.astype(BF16), preferred_element_type=F32)
            m[...] = m_new

    def body(ki, carry):
        tile(ki, False)
        return carry

    lax.fori_loop(0, qi, body, 0)
    tile(qi, True)

    lam = _lambda(lv_ref[...], lam_init)
    o = a1[...] / l1[...] - lam * (a2[...] / l2[...])
    o = o * lax.rsqrt(jnp.mean(o * o, axis=0, keepdims=True) + EPS) * dn_ref[...] * (1.0 - lam_init)
    o_ref[...] = o.T.astype(o_ref.dtype)


def _attn(qt, k16, vt, lv, dnorm_col, n_seq, lam_init, tq_pref=512):
    m = k16.shape[0]
    t_len = m // n_seq
    tq = _row_tile(t_len, tq_pref)
    n_q = t_len // tq
    full = lambda a: pl.BlockSpec(a.shape, lambda b, h, i: (0, 0))
    stat = pltpu.VMEM((1, tq), F32)
    accs = pltpu.VMEM((HEAD_W, tq), F32)
    return pl.pallas_call(
        functools.partial(_attn_kernel, tq=tq, lam_init=lam_init),
        grid=(n_seq, N_HEAD, n_q),
        in_specs=[pl.BlockSpec((HEAD_W, tq), lambda b, h, i: (h, b * n_q + i)),
                  pl.BlockSpec((t_len, HEAD_W), lambda b, h, i: (b, h)),
                  pl.BlockSpec((HEAD_W, t_len), lambda b, h, i: (h, b)),
                  full(lv), full(dnorm_col)],
        out_specs=pl.BlockSpec((tq, HEAD_W), lambda b, h, i: (b * n_q + i, h)),
        out_shape=jax.ShapeDtypeStruct((m, BRANCH_W), BF16),
        scratch_shapes=[pltpu.VMEM((HEAD_W, tq), BF16), pltpu.VMEM((HEAD_W, tq), BF16),
                        stat, stat, accs, stat, stat, accs],
        compiler_params=_cparams(("parallel", "parallel", "arbitrary"), VMEM_LIMIT),
        name="diff_attn",
    )(qt, k16, vt, lv, dnorm_col)


def _ln_swish(y, g, b):
    yc = y - jnp.mean(y, axis=-1, keepdims=True)
    yn = yc * lax.rsqrt(jnp.mean(yc * yc, axis=-1, keepdims=True) + EPS) * g + b
    return _silu(yn)


CARRY_C = 32


def _conf_kernel(u_ref, w_ref, b_ref, lg_ref, lb_ref, o_ref, ext_ref, *, tm):
    t = pl.program_id(1)

    @pl.when(t == 0)
    def _():
        ext_ref[0:CARRY_C, :] = jnp.zeros((CARRY_C, ext_ref.shape[1]), F32)

    ext_ref[CARRY_C:CARRY_C + tm, :] = u_ref[...]
    base = CARRY_C - (CONV_C - 1)
    acc = w_ref[0:1, :] * ext_ref[base:base + tm, :]
    for j in range(1, CONV_C):
        acc = acc + w_ref[j:j + 1, :] * ext_ref[base + j:base + j + tm, :]
    ext_ref[0:CARRY_C, :] = ext_ref[tm:tm + CARRY_C, :]
    o_ref[...] = _ln_swish(acc + b_ref[...], lg_ref[...], lb_ref[...]).astype(o_ref.dtype)


def _conf(u, w, b, lg, lb, n_seq, tm_pref=256):
    m, d = u.shape
    t_len = m // n_seq
    tm = _row_tile(t_len, tm_pref)
    n_t = t_len // tm
    row = pl.BlockSpec((tm, d), lambda s, t: (s * n_t + t, 0))
    full = lambda a: pl.BlockSpec(a.shape, lambda s, t: (0, 0))
    return pl.pallas_call(
        functools.partial(_conf_kernel, tm=tm),
        grid=(n_seq, n_t),
        in_specs=[row, full(w), full(b), full(lg), full(lb)],
        out_specs=row,
        out_shape=jax.ShapeDtypeStruct((m, d), BF16),
        scratch_shapes=[pltpu.VMEM((tm + CARRY_C, d), F32)],
        compiler_params=_cparams(("parallel", "arbitrary")),
        name="conf_conv",
    )(u, w, b, lg, lb)


def _merge_kernel(x_ref, oa_ref, ob_ref, oc_ref, gates_ref, wbr_ref, wo_ref, out_ref):
    mix = None
    for b, o in enumerate((oa_ref, ob_ref, oc_ref)):
        pr = jnp.dot(o[...], wbr_ref[b], preferred_element_type=F32)
        term = gates_ref[:, b * D_MODEL:(b + 1) * D_MODEL] * pr
        mix = term if mix is None else mix + term
    out_ref[...] = x_ref[...] + jnp.dot(mix.astype(BF16), wo_ref[...], preferred_element_type=F32)


def _merge(x, oa, ob, oc, gates, wbr, wo, tm_pref=512):
    m, d = x.shape
    tm = _row_tile(m, tm_pref)
    row = lambda w: pl.BlockSpec((tm, w), lambda i: (i, 0))
    return pl.pallas_call(
        _merge_kernel,
        grid=(m // tm,),
        in_specs=[row(d), row(BRANCH_W), row(BRANCH_W), row(BRANCH_W), row(3 * d),
                  pl.BlockSpec(wbr.shape, lambda i: (0, 0, 0)), pl.BlockSpec(wo.shape, lambda i: (0, 0))],
        out_specs=row(d),
        out_shape=jax.ShapeDtypeStruct((m, d), F32),
        compiler_params=_cparams(("parallel",), VMEM_LIMIT),
        name="merge",
    )(x, oa, ob, oc, gates, wbr, wo)


def _final_norm_kernel(x_ref, g_ref, o_ref):
    o_ref[...] = _rms(x_ref[...], g_ref[...])


def _final_norm(x, g, tm_pref=512):
    m, d = x.shape
    tm = _row_tile(m, tm_pref)
    return pl.pallas_call(
        _final_norm_kernel,
        grid=(m // tm,),
        in_specs=[pl.BlockSpec((tm, d), lambda i: (i, 0)), pl.BlockSpec((1, d), lambda i: (0, 0))],
        out_specs=pl.BlockSpec((tm, d), lambda i: (i, 0)),
        out_shape=jax.ShapeDtypeStruct((m, d), F32),
        compiler_params=_cparams(("parallel",)),
        name="final_norm",
    )(x, g.reshape(1, d))


def _gdn_dec_kernel(qkv_ref, z_ref, ba_ref, st_ref, s0_ref, cw_ref, arow_ref, dtrow_ref, gn_ref,
                    o_ref, st_out_ref, s_out_ref, *, bt):
    qkv_w = N_HEAD * HEAD_W
    x = qkv_ref[...]
    y = cw_ref[CONV_A - 1:CONV_A, :] * x
    for j in range(CONV_A - 1):
        y = y + cw_ref[j:j + 1, :] * st_ref[j]
    for j in range(CONV_A - 2):
        st_out_ref[j] = st_ref[j + 1]
    st_out_ref[CONV_A - 2] = x
    qkv = _silu(y)
    beta, g_all = _gdn_gates(ba_ref[...], arow_ref[...], dtrow_ref[...])
    gn = gn_ref[...]
    pad = jnp.zeros((HEAD_W - bt, HEAD_W), F32)
    for hd in range(N_HEAD):
        qh = _l2norm(qkv[:, hd * HEAD_W:(hd + 1) * HEAD_W]) * (HEAD_W ** -0.5)
        kh = _l2norm(qkv[:, qkv_w + hd * HEAD_W:qkv_w + (hd + 1) * HEAD_W])
        vh = qkv[:, 2 * qkv_w + hd * HEAD_W:2 * qkv_w + (hd + 1) * HEAD_W]
        bcol = beta[:, hd:hd + 1]
        eg = jnp.exp(g_all[:, N_HEAD + hd:N_HEAD + hd + 1])
        qk = jnp.sum(qh * kh, axis=-1, keepdims=True)
        w_t = jnp.concatenate([kh * (bcol * eg), pad], axis=0).T
        q_t = jnp.concatenate([qh * eg, pad], axis=0).T
        k_t = jnp.concatenate([kh, pad], axis=0).T
        for b in range(bt):
            s = s0_ref[b, hd]
            ws = jnp.sum(w_t[:, b:b + 1] * s, axis=0, keepdims=True)
            qs = jnp.sum(q_t[:, b:b + 1] * s, axis=0, keepdims=True)
            v_new = vh[b:b + 1] * bcol[b:b + 1] - ws
            o = qs + qk[b:b + 1] * v_new
            s_out_ref[b, hd] = s * eg[b:b + 1] + k_t[:, b:b + 1] * v_new
            zc = z_ref[b:b + 1, hd * HEAD_W:(hd + 1) * HEAD_W]
            o_ref[b:b + 1, hd * HEAD_W:(hd + 1) * HEAD_W] = _rms(o, gn) * _silu(zc)


def _gdn_dec(qkv, z, ba, st, s0, conv_w, arow, dtrow, gnorm, bt=8):
    n, w = qkv.shape
    row = lambda wd: pl.BlockSpec((bt, wd), lambda i: (i, 0))
    full = lambda a: pl.BlockSpec(a.shape, lambda i: (0, 0))
    st_spec = pl.BlockSpec((CONV_A - 1, bt, w), lambda i: (0, i, 0))
    s_spec = pl.BlockSpec((bt, N_HEAD, HEAD_W, HEAD_W), lambda i: (i, 0, 0, 0))
    return pl.pallas_call(
        functools.partial(_gdn_dec_kernel, bt=bt),
        grid=(n // bt,),
        in_specs=[row(w), row(BRANCH_W), row(HEAD_W), st_spec, s_spec,
                  full(conv_w), full(arow), full(dtrow), full(gnorm)],
        out_specs=[row(BRANCH_W), st_spec, s_spec],
        out_shape=[jax.ShapeDtypeStruct((n, BRANCH_W), F32),
                   jax.ShapeDtypeStruct(st.shape, F32),
                   jax.ShapeDtypeStruct(s0.shape, F32)],
        compiler_params=_cparams(("parallel",)),
        name="gdn_decode",
    )(qkv, z, ba, st, s0, conv_w, arow, dtrow, gnorm)


def _attn_dec_kernel(pt_ref, q_ref, kn_ref, vn_ref, lv_ref, dn_ref, *rest, n_page, lam_init):
    k_refs = rest[:n_page]
    v_refs = rest[n_page:2 * n_page]
    o_ref, m_ref, l_ref, a_ref = rest[2 * n_page:]
    j = pl.program_id(1)

    @pl.when(j == 0)
    def _():
        m_ref[...] = jnp.full(m_ref.shape, NEG_INF, F32)
        l_ref[...] = jnp.zeros(l_ref.shape, F32)
        a_ref[...] = jnp.zeros(a_ref.shape, F32)

    q4 = q_ref[0]
    lane = lax.broadcasted_iota(jnp.int32, q4.shape, 1)
    qmat = jnp.concatenate([jnp.where(lane < HALF_W, q4, 0.0), jnp.where(lane >= HALF_W, q4, 0.0)], axis=0)
    qmat16 = qmat.astype(BF16)
    rows = PAGE * N_HEAD
    r = lax.broadcasted_iota(jnp.int32, (2 * N_HEAD, n_page * rows), 0)
    c = lax.broadcasted_iota(jnp.int32, (2 * N_HEAD, n_page * rows), 1)
    valid = (c % N_HEAD) == (r % N_HEAD)
    sc = jnp.concatenate(
        [lax.dot_general(qmat16, k_refs[p][...].astype(BF16), (((1,), (1,)), ((), ())),
                         preferred_element_type=F32) for p in range(n_page)], axis=1)
    sc = jnp.where(valid, sc, NEG_INF)
    m_old = m_ref[...]
    m_new = jnp.maximum(m_old, jnp.max(sc, axis=-1, keepdims=True))
    alpha = jnp.exp(m_old - m_new)
    pr = jnp.exp(sc - m_new)
    l_ref[...] = alpha * l_ref[...] + jnp.sum(pr, axis=-1, keepdims=True)
    pr = pr.astype(BF16)
    pv = None
    for p in range(n_page):
        term = jnp.dot(pr[:, p * rows:(p + 1) * rows], v_refs[p][...].astype(BF16), preferred_element_type=F32)
        pv = term if pv is None else pv + term
    a_ref[...] = alpha * a_ref[...] + pv
    m_ref[...] = m_new

    @pl.when(j == pl.num_programs(1) - 1)
    def _():
        k8 = jnp.concatenate([kn_ref[0], kn_ref[0]], axis=0)
        v8 = jnp.concatenate([vn_ref[0], vn_ref[0]], axis=0)
        s_self = jnp.sum(qmat * k8, axis=-1, keepdims=True)
        m_new = jnp.maximum(m_ref[...], s_self)
        alpha = jnp.exp(m_ref[...] - m_new)
        p_self = jnp.exp(s_self - m_new)
        l_fin = alpha * l_ref[...] + p_self
        a_fin = alpha * a_ref[...] + p_self * v8
        on = a_fin / l_fin
        lam = _lambda(lv_ref[...], lam_init)
        o = on[0:N_HEAD] - lam * on[N_HEAD:2 * N_HEAD]
        o_ref[0] = _rms(o, dn_ref[...]) * (1.0 - lam_init)


def _attn_dec(q, k_new, v_new, cache_k, cache_v, page_flat, layer, lv, dnorm, lam_init, n_page=8):
    n = q.shape[0]
    pages_per_seq = page_flat.shape[0] // n
    n_page = n_page if pages_per_seq % n_page == 0 else 1
    tok = pl.BlockSpec((1, N_HEAD, HEAD_W), lambda b, j, pt: (b, 0, 0))
    full = lambda a: pl.BlockSpec(a.shape, lambda b, j, pt: (0, 0))

    def page_spec(p):
        return pl.BlockSpec((None, None, PAGE * N_HEAD, HEAD_W),
                            lambda b, j, pt: (layer, pt[b * pages_per_seq + j * n_page + p], 0, 0))

    return pl.pallas_call(
        functools.partial(_attn_dec_kernel, n_page=n_page, lam_init=lam_init),
        grid_spec=pltpu.PrefetchScalarGridSpec(
            num_scalar_prefetch=1,
            grid=(n, pages_per_seq // n_page),
            in_specs=[tok, tok, tok, full(lv), full(dnorm)]
                     + [page_spec(p) for p in range(n_page)] * 2,
            out_specs=tok,
            scratch_shapes=[pltpu.VMEM((2 * N_HEAD, 1), F32), pltpu.VMEM((2 * N_HEAD, 1), F32),
                            pltpu.VMEM((2 * N_HEAD, HEAD_W), F32)],
        ),
        out_shape=jax.ShapeDtypeStruct((n, N_HEAD, HEAD_W), F32),
        compiler_params=_cparams(("parallel", "arbitrary")),
        name="diff_attn_decode",
    )(page_flat, q.reshape(n, N_HEAD, HEAD_W), k_new.reshape(n, N_HEAD, HEAD_W),
      v_new.reshape(n, N_HEAD, HEAD_W), lv, dnorm,
      *([cache_k] * n_page), *([cache_v] * n_page))


def _conf_dec_kernel(u_ref, st_ref, w_ref, b_ref, lg_ref, lb_ref, o_ref, st_out_ref):
    u = u_ref[...]
    acc = w_ref[CONV_C - 1:CONV_C, :] * u
    for j in range(CONV_C - 1):
        acc = acc + w_ref[j:j + 1, :] * st_ref[j]
    for j in range(CONV_C - 2):
        st_out_ref[j] = st_ref[j + 1]
    st_out_ref[CONV_C - 2] = u
    o_ref[...] = _ln_swish(acc + b_ref[...], lg_ref[...], lb_ref[...])


def _conf_dec(u, st, w, b, lg, lb):
    vm = pl.BlockSpec(memory_space=pltpu.VMEM)
    return pl.pallas_call(
        _conf_dec_kernel,
        in_specs=[vm] * 6,
        out_specs=[vm, vm],
        out_shape=[jax.ShapeDtypeStruct(u.shape, F32), jax.ShapeDtypeStruct(st.shape, F32)],
        name="conf_conv_decode",
    )(u, st, w, b, lg, lb)


def _layer_weights(l, p):
    qk_a = N_HEAD * HEAD_W
    sizes = (3 * qk_a, BRANCH_W, N_HEAD, N_HEAD, BRANCH_W, BRANCH_W, BRANCH_W, 2 * BRANCH_W, 3 * D_MODEL)
    offs = np.concatenate([[0], np.cumsum(sizes)])
    w_in = p["w_in"][l]
    col = lambda i: w_in[:, offs[i]:offs[i + 1]].astype(BF16)
    wba = jnp.pad(jnp.concatenate([w_in[:, offs[2]:offs[3]], w_in[:, offs[3]:offs[4]]], axis=1),
                  ((0, 0), (0, HEAD_W - 2 * N_HEAD))).astype(BF16)
    glu = col(7)
    lane_pad = lambda v: jnp.pad(v.reshape(1, N_HEAD), ((0, 0), (N_HEAD, HEAD_W - 2 * N_HEAD)))
    r2 = lambda v: v.reshape(1, -1)
    return dict(
        n1=p["norm_ffn1"][l], f1i=p["w_ffn1_in"][l].astype(BF16), f1o=p["w_ffn1_out"][l].astype(BF16),
        nm=r2(p["norm_mix"][l]),
        wqkv=col(0), wz=col(1), wba=wba, wq=col(4), wk=col(5), wv=col(6),
        wqt=col(4).T, wvt=col(6).T, dnc=p["diff_norm"][l].reshape(-1, 1),
        wga=glu[:, :BRANCH_W], wgb=glu[:, BRANCH_W:], wgate=col(8),
        cw=p["gdn_conv_w"][l], arow=lane_pad(p["gdn_a_log"][l]), dtrow=lane_pad(p["gdn_dt_bias"][l]),
        gn=r2(p["gdn_norm"][l]), lv=p["diff_lambda"][l], dn=r2(p["diff_norm"][l]),
        dw=p["conv_dw_w"][l], db=r2(p["conv_dw_b"][l]), lg=r2(p["conv_ln_g"][l]), lb=r2(p["conv_ln_b"][l]),
        wbr=p["w_branch"][l].astype(BF16), wo=p["w_out"][l].astype(BF16),
        n2=p["norm_ffn2"][l], f2i=p["w_ffn2_in"][l].astype(BF16), f2o=p["w_ffn2_out"][l].astype(BF16),
    )


def _prompt_layer(x, w, rope, n_seq, lam_init):
    t_len = x.shape[0] // n_seq
    x = _ffn(x, w["n1"], w["f1i"], w["f1o"])
    qkv, z, ba = _proj_gdn(x, w["nm"], w["wqkv"], w["wz"], w["wba"])
    qt, k, k16, v, vt = _proj_attn(x, w["nm"], w["wqt"], w["wk"], w["wv"], w["wvt"], *rope, t_len)
    u, gates = _proj_cg(x, w["nm"], w["wga"], w["wgb"], w["wgate"])
    o_a, s_new = _gdn(qkv, z, ba, w["cw"], w["arow"], w["dtrow"], w["gn"], n_seq)
    o_b = _attn(qt, k16, vt, w["lv"], w["dnc"], n_seq, lam_init)
    o_c = _conf(u, w["dw"], w["db"], w["lg"], w["lb"], n_seq)
    x = _merge(x, o_a, o_b, o_c, gates, w["wbr"], w["wo"])
    x = _ffn(x, w["n2"], w["f2i"], w["f2o"])
    gconv = qkv.reshape(n_seq, t_len, -1)[:, t_len - (CONV_A - 1):]
    cconv = u.reshape(n_seq, t_len, -1)[:, t_len - (CONV_C - 1):]
    return x, k, v, s_new, gconv, cconv


def _sample_layer(x, w, cos, sin, cache_k, cache_v, page_flat, layer, s0, gst, cst, lam_init):
    n = x.shape[0]
    x = _ffn(x, w["n1"], w["f1i"], w["f1o"])
    qkv, z, ba = _proj_gdn(x, w["nm"], w["wqkv"], w["wz"], w["wba"])
    q, k, v = _proj_attn_dec(x, w["nm"], w["wq"], w["wk"], w["wv"], cos, sin)
    u, gates = _proj_cg(x, w["nm"], w["wga"], w["wgb"], w["wgate"])
    o_a, gst_new, s_new = _gdn_dec(qkv, z, ba, jnp.swapaxes(gst, 0, 1), s0,
                                   w["cw"], w["arow"], w["dtrow"], w["gn"])
    o_b = _attn_dec(q, k, v, cache_k, cache_v, page_flat, layer, w["lv"], w["dn"], lam_init)
    o_c, cst_new = _conf_dec(u, jnp.swapaxes(cst, 0, 1), w["dw"], w["db"], w["lg"], w["lb"])
    x = _merge(x, o_a.astype(BF16), o_b.reshape(n, BRANCH_W).astype(BF16), o_c.astype(BF16),
               gates, w["wbr"], w["wo"])
    x = _ffn(x, w["n2"], w["f2i"], w["f2o"])
    return x, k, v, s_new, jnp.swapaxes(gst_new, 0, 1), jnp.swapaxes(cst_new, 0, 1)


def kernel(x_prompt, x_sample, cache_k, cache_v, page_table, state_gdn, state_gdn_conv, state_conv,
           norm_ffn1, w_ffn1_in, w_ffn1_out, norm_mix, w_in, gdn_conv_w, gdn_a_log, gdn_dt_bias,
           gdn_norm, diff_lambda, diff_norm, conv_dw_w, conv_dw_b, conv_ln_g, conv_ln_b,
           w_branch, w_out, norm_ffn2, w_ffn2_in, w_ffn2_out, norm_final):
    params = dict(norm_ffn1=norm_ffn1, w_ffn1_in=w_ffn1_in, w_ffn1_out=w_ffn1_out, norm_mix=norm_mix,
                  w_in=w_in, gdn_conv_w=gdn_conv_w, gdn_a_log=gdn_a_log, gdn_dt_bias=gdn_dt_bias,
                  gdn_norm=gdn_norm, diff_lambda=diff_lambda, diff_norm=diff_norm, conv_dw_w=conv_dw_w,
                  conv_dw_b=conv_dw_b, conv_ln_g=conv_ln_g, conv_ln_b=conv_ln_b, w_branch=w_branch,
                  w_out=w_out, norm_ffn2=norm_ffn2, w_ffn2_in=w_ffn2_in, w_ffn2_out=w_ffn2_out)
    depth = w_in.shape[0]
    bp, tp, d = x_prompt.shape
    bs, ts, _ = x_sample.shape
    past_len = page_table.shape[1] * PAGE
    n_pool = cache_k.shape[1]

    rope_p = _rope_table(0, tp)
    cos_s, sin_s, _, _ = _rope_table(past_len, 8)
    cos_s = jnp.broadcast_to(cos_s[0:1], (bs, HEAD_W))
    sin_s = jnp.broadcast_to(sin_s[0:1], (bs, HEAD_W))
    ck = cache_k.reshape(depth, n_pool, PAGE * N_HEAD, HEAD_W)
    cv = cache_v.reshape(depth, n_pool, PAGE * N_HEAD, HEAD_W)
    page_flat = page_table.reshape(-1)

    xp = x_prompt.reshape(bp * tp, d)
    xs = x_sample.reshape(bs * ts, d)
    outs = [[] for _ in range(10)]
    for l in range(depth):
        lam_init = 0.8 - 0.6 * math.exp(-0.3 * l)
        w = _layer_weights(l, params)
        xp, k_p, v_p, s_p, gc_p, c_p = _prompt_layer(xp, w, rope_p, bp, lam_init)
        xs, k_s, v_s, s_s, gc_s, c_s = _sample_layer(xs, w, cos_s, sin_s, ck, cv, page_flat, l,
                                                     state_gdn[l], state_gdn_conv[l], state_conv[l], lam_init)
        vals = (k_p.reshape(bp, tp, N_HEAD, HEAD_W), v_p.reshape(bp, tp, N_HEAD, HEAD_W),
                k_s.reshape(bs, ts, N_HEAD, HEAD_W), v_s.reshape(bs, ts, N_HEAD, HEAD_W),
                s_p, s_s, gc_p, gc_s, c_p, c_s)
        for lst, val in zip(outs, vals):
            lst.append(val)
    y_prompt = _final_norm(xp, norm_final).reshape(bp, tp, d)
    y_sample = _final_norm(xs, norm_final).reshape(bs, ts, d)
    return (y_prompt, y_sample) + tuple(jnp.stack(o) for o in outs)
```

```python
import functools
import math

import jax
import jax.numpy as jnp
import numpy as np
from jax import lax
from jax.experimental import pallas as pl
from jax.experimental.pallas import tpu as pltpu

F32 = jnp.float32
BF16 = jnp.bfloat16

D_MODEL = 1024
N_HEAD = 4
HEAD_W = 128
HALF_W = 64
BRANCH_W = N_HEAD * HEAD_W
CONV_A = 4
CHUNK = 64
CONV_C = 31
D_FF = 2816
EPS = 1e-6
ROPE_THETA = 10000.0
PAGE = 128
NEG_INF = float("-inf")

VMEM_LIMIT = 56 * 1024 * 1024


def _cparams(sem, vmem=None):
    return pltpu.CompilerParams(dimension_semantics=sem, vmem_limit_bytes=vmem)


def _rms(x, g):
    return x * lax.rsqrt(jnp.mean(x * x, axis=-1, keepdims=True) + EPS) * g


def _sigmoid(x):
    return 1.0 / (1.0 + jnp.exp(-x))


def _silu(x):
    return x * _sigmoid(x)


def _softplus(x):
    return jnp.maximum(x, 0.0) + jnp.log1p(jnp.exp(-jnp.abs(x)))


def _row_tile(m, pref):
    return pref if m % pref == 0 else m


FF_CHUNK = 256


def _ffn_weights(w_in, w_out):
    d = w_in.shape[0]
    n_c = D_FF // FF_CHUNK
    gate = w_in[:, :D_FF].reshape(d, n_c, 1, FF_CHUNK)
    up = w_in[:, D_FF:].reshape(d, n_c, 1, FF_CHUNK)
    return jnp.concatenate([gate, up], axis=2).reshape(d, 2 * D_FF).astype(BF16), w_out.astype(BF16)


def _ffn_kernel(x_ref, g_ref, wi_ref, wo_ref, o_ref):
    x = x_ref[...]
    h = _rms(x, g_ref[...]).astype(BF16)
    acc = None
    for c in range(D_FF // FF_CHUNK):
        r = jnp.dot(h, wi_ref[:, 2 * c * FF_CHUNK:2 * (c + 1) * FF_CHUNK], preferred_element_type=F32)
        act = (_silu(r[:, :FF_CHUNK]) * r[:, FF_CHUNK:]).astype(BF16)
        term = jnp.dot(act, wo_ref[c * FF_CHUNK:(c + 1) * FF_CHUNK, :], preferred_element_type=F32)
        acc = term if acc is None else acc + term
    o_ref[...] = x + 0.5 * acc


def _ffn(x, g, w_in, w_out, tm_pref=512):
    m, d = x.shape
    tm = _row_tile(m, tm_pref)
    resident = lambda a: pl.BlockSpec(a.shape, lambda i: (0, 0), pipeline_mode=pl.Buffered(1))
    return pl.pallas_call(
        _ffn_kernel,
        grid=(m // tm,),
        in_specs=[pl.BlockSpec((tm, d), lambda i: (i, 0)), pl.BlockSpec((1, d), lambda i: (0, 0)),
                  resident(w_in), resident(w_out)],
        out_specs=pl.BlockSpec((tm, d), lambda i: (i, 0)),
        out_shape=jax.ShapeDtypeStruct((m, d), F32),
        compiler_params=_cparams(("parallel",), VMEM_LIMIT),
        name="ffn",
    )(x, g.reshape(1, d), w_in, w_out)


def _proj_gdn_kernel(x_ref, g_ref, wqkv_ref, wz_ref, wba_ref, qkv_ref, z_ref, ba_ref):
    h = _rms(x_ref[...], g_ref[...]).astype(BF16)
    qkv_ref[...] = jnp.dot(h, wqkv_ref[...], preferred_element_type=F32)
    z_ref[...] = jnp.dot(h, wz_ref[...], preferred_element_type=F32)
    ba_ref[...] = jnp.dot(h, wba_ref[...], preferred_element_type=F32)


def _proj_gdn(x, g, wqkv, wz, wba, tm_pref=512):
    m, d = x.shape
    tm = _row_tile(m, tm_pref)
    row = lambda w: pl.BlockSpec((tm, w), lambda i: (i, 0))
    full = lambda a: pl.BlockSpec(a.shape, lambda i: (0, 0))
    return pl.pallas_call(
        _proj_gdn_kernel,
        grid=(m // tm,),
        in_specs=[row(d), full(g), full(wqkv), full(wz), full(wba)],
        out_specs=[row(wqkv.shape[1]), row(wz.shape[1]), row(wba.shape[1])],
        out_shape=[jax.ShapeDtypeStruct((m, w.shape[1]), F32) for w in (wqkv, wz, wba)],
        compiler_params=_cparams(("parallel",), VMEM_LIMIT),
        name="proj_gdn",
    )(x, g, wqkv, wz, wba)


def _rope_table_kernel(inv_ref, invc_ref, cos_ref, sin_ref, cost_ref, sint_ref, *, pos0, tm):
    i = pl.program_id(0)
    half = HALF_W // 2
    pos = lax.broadcasted_iota(jnp.int32, (tm, HEAD_W), 0) + (i * tm + pos0)
    ang = pos.astype(F32) * inv_ref[...]
    first = (lax.broadcasted_iota(jnp.int32, (tm, HEAD_W), 1) % HALF_W) < half
    s = jnp.sin(ang)
    cos_ref[...] = jnp.cos(ang)
    sin_ref[...] = jnp.where(first, -s, s)
    pos_t = lax.broadcasted_iota(jnp.int32, (HEAD_W, tm), 1) + (i * tm + pos0)
    ang_t = pos_t.astype(F32) * invc_ref[...]
    first_t = (lax.broadcasted_iota(jnp.int32, (HEAD_W, tm), 0) % HALF_W) < half
    s_t = jnp.sin(ang_t)
    cost_ref[...] = jnp.cos(ang_t)
    sint_ref[...] = jnp.where(first_t, -s_t, s_t)


def _rope_table(pos0, n):
    half = HALF_W // 2
    inv = ROPE_THETA ** (-jnp.arange(half, dtype=F32) / half)
    inv_rep = jnp.tile(inv, HEAD_W // half)
    tm = _row_tile(n, 512)
    return pl.pallas_call(
        functools.partial(_rope_table_kernel, pos0=pos0, tm=tm),
        grid=(n // tm,),
        in_specs=[pl.BlockSpec((1, HEAD_W), lambda i: (0, 0)), pl.BlockSpec((HEAD_W, 1), lambda i: (0, 0))],
        out_specs=[pl.BlockSpec((tm, HEAD_W), lambda i: (i, 0))] * 2
                  + [pl.BlockSpec((HEAD_W, tm), lambda i: (0, i))] * 2,
        out_shape=[jax.ShapeDtypeStruct((n, HEAD_W), F32)] * 2 + [jax.ShapeDtypeStruct((HEAD_W, n), F32)] * 2,
        compiler_params=_cparams(("parallel",)),
        name="rope_table",
    )(inv_rep.reshape(1, HEAD_W), inv_rep.reshape(HEAD_W, 1))


def _rope_rows(p, cos, sin, scale, out_refs):
    half = HALF_W // 2
    first = (lax.broadcasted_iota(jnp.int32, cos.shape, 1) % HALF_W) < half
    for hd in range(N_HEAD):
        xh = p[:, hd * HEAD_W:(hd + 1) * HEAD_W]
        partner = jnp.where(first, pltpu.roll(xh, HEAD_W - half, 1), pltpu.roll(xh, half, 1))
        y = (xh * cos + partner * sin) * scale
        for o in out_refs:
            if len(o.shape) == 3:
                o[:, hd, :] = y.astype(o.dtype)
            else:
                o[:, hd * HEAD_W:(hd + 1) * HEAD_W] = y.astype(o.dtype)


def _proj_attn_dec_kernel(x_ref, g_ref, wq_ref, wk_ref, wv_ref, cos_ref, sin_ref, q_ref, k_ref, v_ref):
    h = _rms(x_ref[...], g_ref[...]).astype(BF16)
    cos = cos_ref[...]
    sin = sin_ref[...]
    _rope_rows(jnp.dot(h, wq_ref[...], preferred_element_type=F32), cos, sin, HALF_W ** -0.5, (q_ref,))
    _rope_rows(jnp.dot(h, wk_ref[...], preferred_element_type=F32), cos, sin, 1.0, (k_ref,))
    v_ref[...] = jnp.dot(h, wv_ref[...], preferred_element_type=F32)


def _proj_attn_dec(x, g, wq, wk, wv, cos, sin):
    vm = pl.BlockSpec(memory_space=pltpu.VMEM)
    m = x.shape[0]
    return pl.pallas_call(
        _proj_attn_dec_kernel,
        in_specs=[vm] * 7,
        out_specs=[vm] * 3,
        out_shape=[jax.ShapeDtypeStruct((m, BRANCH_W), F32)] * 3,
        compiler_params=pltpu.CompilerParams(vmem_limit_bytes=VMEM_LIMIT),
        name="proj_attn_decode",
    )(x, g, wq, wk, wv, cos, sin)


Q_SCALE_LOG2 = (HALF_W ** -0.5) * math.log2(math.e)


def _proj_attn_kernel(x_ref, g_ref, wqt_ref, wk_ref, wv_ref, wvt_ref, cos_ref, sin_ref, cost_ref, sint_ref,
                      qt_ref, k_ref, k16_ref, v_ref, vt_ref):
    h = _rms(x_ref[...], g_ref[...]).astype(BF16)
    nt = (((1,), (1,)), ((), ()))
    _rope_rows(jnp.dot(h, wk_ref[...], preferred_element_type=F32), cos_ref[...], sin_ref[...], 1.0,
               (k_ref, k16_ref))
    v = jnp.dot(h, wv_ref[...], preferred_element_type=F32)
    for hd in range(N_HEAD):
        v_ref[:, hd, :] = v[:, hd * HEAD_W:(hd + 1) * HEAD_W]
    vt_ref[...] = lax.dot_general(wvt_ref[...], h, nt, preferred_element_type=F32).astype(vt_ref.dtype)
    qt = lax.dot_general(wqt_ref[...], h, nt, preferred_element_type=F32)
    cos_t = cost_ref[...]
    sin_t = sint_ref[...]
    half = HALF_W // 2
    for hd in range(N_HEAD):
        xh = qt[hd * HEAD_W:(hd + 1) * HEAD_W]
        partner = jnp.concatenate([xh[half:2 * half], xh[0:half], xh[3 * half:4 * half], xh[2 * half:3 * half]],
                                  axis=0)
        qt_ref[hd * HEAD_W:(hd + 1) * HEAD_W, :] = ((xh * cos_t + partner * sin_t) * Q_SCALE_LOG2).astype(qt_ref.dtype)


def _proj_attn(x, g, wqt, wk, wv, wvt, cos, sin, cos_t, sin_t, rows_per_seq, tm_pref=512):
    m, d = x.shape
    tm = _row_tile(rows_per_seq, tm_pref)
    n_t = rows_per_seq // tm
    row = lambda w: pl.BlockSpec((tm, w), lambda i: (i, 0))
    row3 = pl.BlockSpec((tm, N_HEAD, HEAD_W), lambda i: (i, 0, 0))
    colb = pl.BlockSpec((BRANCH_W, tm), lambda i: (0, i))
    full = lambda a: pl.BlockSpec(a.shape, lambda i: (0, 0))
    tab = pl.BlockSpec((tm, HEAD_W), lambda i: (i % n_t, 0))
    tab_t = pl.BlockSpec((HEAD_W, tm), lambda i: (0, i % n_t))
    return pl.pallas_call(
        _proj_attn_kernel,
        grid=(m // tm,),
        in_specs=[row(d), full(g), full(wqt), full(wk), full(wv), full(wvt), tab, tab, tab_t, tab_t],
        out_specs=[colb, row3, row(BRANCH_W), row3, colb],
        out_shape=[jax.ShapeDtypeStruct((BRANCH_W, m), BF16),
                   jax.ShapeDtypeStruct((m, N_HEAD, HEAD_W), F32),
                   jax.ShapeDtypeStruct((m, BRANCH_W), BF16),
                   jax.ShapeDtypeStruct((m, N_HEAD, HEAD_W), F32),
                   jax.ShapeDtypeStruct((BRANCH_W, m), BF16)],
        compiler_params=_cparams(("parallel",), VMEM_LIMIT),
        name="proj_attn",
    )(x, g, wqt, wk, wv, wvt, cos, sin, cos_t, sin_t)


def _proj_cg_kernel(x_ref, g_ref, wa_ref, wb_ref, wgate_ref, u_ref, gates_ref):
    h = _rms(x_ref[...], g_ref[...]).astype(BF16)
    a = jnp.dot(h, wa_ref[...], preferred_element_type=F32)
    b = jnp.dot(h, wb_ref[...], preferred_element_type=F32)
    u_ref[...] = a * _sigmoid(b)
    gates_ref[...] = _sigmoid(jnp.dot(h, wgate_ref[...], preferred_element_type=F32)).astype(gates_ref.dtype)


def _proj_cg(x, g, wa, wb, wgate, tm_pref=512):
    m, d = x.shape
    tm = _row_tile(m, tm_pref)
    row = lambda w: pl.BlockSpec((tm, w), lambda i: (i, 0))
    full = lambda a: pl.BlockSpec(a.shape, lambda i: (0, 0))
    return pl.pallas_call(
        _proj_cg_kernel,
        grid=(m // tm,),
        in_specs=[row(d), full(g), full(wa), full(wb), full(wgate)],
        out_specs=[row(wa.shape[1]), row(wgate.shape[1])],
        out_shape=[jax.ShapeDtypeStruct((m, wa.shape[1]), F32),
                   jax.ShapeDtypeStruct((m, wgate.shape[1]), BF16)],
        compiler_params=_cparams(("parallel",), VMEM_LIMIT),
        name="proj_cg",
    )(x, g, wa, wb, wgate)


def _l2norm(x):
    return x * lax.rsqrt(jnp.sum(x * x, axis=-1, keepdims=True) + EPS)


def _split2(x):
    hi = x.astype(BF16)
    return hi, (x - hi.astype(F32)).astype(BF16)


def _split3(x):
    hi = x.astype(BF16)
    r1 = x - hi.astype(F32)
    mid = r1.astype(BF16)
    return hi, mid, (r1 - mid.astype(F32)).astype(BF16)


def _dot3(a, b):
    a_hi, a_lo = _split2(a)
    b_hi, b_lo = _split2(b)
    return (jnp.dot(a_hi, b_hi, preferred_element_type=F32) + jnp.dot(a_hi, b_lo, preferred_element_type=F32)
            + jnp.dot(a_lo, b_hi, preferred_element_type=F32))


def _gdn_gates(ba, arow, dtrow):
    beta = _sigmoid(ba)
    g = -jnp.exp(arow) * _softplus(ba + dtrow)
    return beta, g


def _gdn_kernel(qkv_ref, z_ref, ba_ref, cw_ref, arow_ref, dtrow_ref, gn_ref,
                o_ref, s_out_ref, ext_ref, s_ref, *, tm):
    t = pl.program_id(1)
    nc = tm // CHUNK
    qkv_w = N_HEAD * HEAD_W

    @pl.when(t == 0)
    def _():
        ext_ref[0:8, :] = jnp.zeros((8, ext_ref.shape[1]), F32)
        s_ref[...] = jnp.zeros(s_ref.shape, F32)

    ext_ref[8:8 + tm, :] = qkv_ref[...]
    y = cw_ref[CONV_A - 1:CONV_A, :] * ext_ref[8:8 + tm, :]
    for j in range(CONV_A - 1):
        off = 8 - (CONV_A - 1) + j
        y = y + cw_ref[j:j + 1, :] * ext_ref[off:off + tm, :]
    ext_ref[0:8, :] = ext_ref[tm:tm + 8, :]
    qkv = _silu(y)

    beta, g_all = _gdn_gates(ba_ref[...], arow_ref[...], dtrow_ref[...])
    r = lax.broadcasted_iota(jnp.int32, (tm, tm), 0)
    c = lax.broadcasted_iota(jnp.int32, (tm, tm), 1)
    same = r // CHUNK == c // CHUNK
    lbd = jnp.where(same, jnp.where(c <= r, 1.0, 0.0), 0.0).astype(BF16)
    gc = None
    for piece in _split3(g_all):
        term = jnp.dot(lbd, piece, preferred_element_type=F32)
        gc = term if gc is None else gc + term
    gc_t = gc.T

    ri = lax.broadcasted_iota(jnp.int32, (CHUNK, tm), 0)
    rc = lax.broadcasted_iota(jnp.int32, (CHUNK, tm), 1)
    eye_row = jnp.where(ri == rc % CHUNK, 1.0, 0.0)

    def to_bd(x_row):
        return jnp.where(same, jnp.concatenate([x_row] * nc, axis=0), 0.0)

    def from_bd(x_bd):
        acc = x_bd[0:CHUNK]
        for ch in range(1, nc):
            acc = acc + x_bd[ch * CHUNK:(ch + 1) * CHUNK]
        return acc

    gn = gn_ref[...]
    nt = (((1,), (1,)), ((), ()))
    heads = range(N_HEAD)
    qh = [_l2norm(qkv[:, hd * HEAD_W:(hd + 1) * HEAD_W]) * (HEAD_W ** -0.5) for hd in heads]
    kh = [_l2norm(qkv[:, qkv_w + hd * HEAD_W:qkv_w + (hd + 1) * HEAD_W]) for hd in heads]
    vh = [qkv[:, 2 * qkv_w + hd * HEAD_W:2 * qkv_w + (hd + 1) * HEAD_W] for hd in heads]
    gcol = [gc[:, N_HEAD + hd:N_HEAD + hd + 1] for hd in heads]
    grow = [gc_t[N_HEAD + hd:N_HEAD + hd + 1, :] for hd in heads]
    bcol = [beta[:, hd:hd + 1] for hd in heads]
    egc = [jnp.exp(g) for g in gcol]
    k16 = [k.astype(BF16) for k in kh]
    lower = jnp.logical_and(same, r >= c)
    decay = [jnp.exp(jnp.where(lower, gcol[hd] - grow[hd], NEG_INF)) for hd in heads]
    kk = [lax.dot_general(k16[hd], k16[hd], nt, preferred_element_type=F32) for hd in heads]
    qk = [lax.dot_general(qh[hd].astype(BF16), k16[hd], nt, preferred_element_type=F32) for hd in heads]
    n_bd = [jnp.where(r > c, -(kk[hd] * decay[hd] * bcol[hd]), 0.0) for hd in heads]
    attn16 = [(qk[hd] * decay[hd]).astype(BF16) for hd in heads]
    n_row = [from_bd(n) for n in n_bd]
    s_inv = [eye_row + n for n in n_row]
    x_row = [_dot3(n_row[hd], n_bd[hd]) for hd in heads]
    for _ in range(4):
        res = [_dot3(jnp.concatenate([s_inv[hd], x_row[hd]], axis=0), to_bd(x_row[hd])) for hd in heads]
        s_inv = [s_inv[hd] + res[hd][0:CHUNK] for hd in heads]
        x_row = [res[hd][CHUNK:2 * CHUNK] for hd in heads]
    s_inv = [s_inv[hd] + _dot3(s_inv[hd], to_bd(x_row[hd])) for hd in heads]
    sol = [_dot3(to_bd(s_inv[hd]), jnp.concatenate([vh[hd] * bcol[hd], kh[hd] * (bcol[hd] * egc[hd])], axis=1))
           for hd in heads]
    qg16 = [(qh[hd] * egc[hd]).astype(BF16) for hd in heads]
    w16 = [sol[hd][:, HEAD_W:].astype(BF16) for hd in heads]
    s = [s_ref[hd] for hd in heads]
    v_news = [[] for _ in heads]
    o_inters = [[] for _ in heads]
    for ch in range(nc):
        sl = slice(ch * CHUNK, (ch + 1) * CHUNK)
        for hd in heads:
            g_last = gcol[hd][ch * CHUNK + CHUNK - 1:(ch + 1) * CHUNK]
            res = jnp.dot(jnp.concatenate([w16[hd][sl], qg16[hd][sl]], axis=0), s[hd].astype(BF16),
                          preferred_element_type=F32)
            v_new = sol[hd][sl, :HEAD_W] - res[0:CHUNK]
            o_inters[hd].append(res[CHUNK:2 * CHUNK])
            v_news[hd].append(v_new)
            kd16 = (kh[hd][sl] * jnp.exp(g_last - gcol[hd][sl])).astype(BF16)
            s[hd] = s[hd] * jnp.exp(g_last) + lax.dot_general(kd16, v_new.astype(BF16), (((0,), (0,)), ((), ())),
                                                              preferred_element_type=F32)
    for hd in heads:
        s_ref[hd] = s[hd]
        o = jnp.concatenate(o_inters[hd], axis=0) + jnp.dot(
            attn16[hd], jnp.concatenate(v_news[hd], axis=0).astype(BF16), preferred_element_type=F32)
        zc = z_ref[:, hd * HEAD_W:(hd + 1) * HEAD_W]
        o_ref[:, hd * HEAD_W:(hd + 1) * HEAD_W] = (_rms(o, gn) * _silu(zc)).astype(o_ref.dtype)

    @pl.when(t == pl.num_programs(1) - 1)
    def _():
        s_out_ref[0] = s_ref[...]


def _gdn(qkv, z, ba, conv_w, arow, dtrow, gnorm, n_seq, tm_pref=256):
    m, w = qkv.shape
    t_len = m // n_seq
    tm = _row_tile(t_len, tm_pref)
    n_t = t_len // tm
    row = lambda wd: pl.BlockSpec((tm, wd), lambda b, t: (b * n_t + t, 0))
    full = lambda a: pl.BlockSpec(a.shape, lambda b, t: (0, 0))
    return pl.pallas_call(
        functools.partial(_gdn_kernel, tm=tm),
        grid=(n_seq, n_t),
        in_specs=[row(w), row(BRANCH_W), row(HEAD_W), full(conv_w), full(arow), full(dtrow), full(gnorm)],
        out_specs=[row(BRANCH_W),
                   pl.BlockSpec((1, N_HEAD, HEAD_W, HEAD_W), lambda b, t: (b, 0, 0, 0))],
        out_shape=[jax.ShapeDtypeStruct((m, BRANCH_W), BF16),
                   jax.ShapeDtypeStruct((n_seq, N_HEAD, HEAD_W, HEAD_W), F32)],
        scratch_shapes=[pltpu.VMEM((tm + 8, w), F32), pltpu.VMEM((N_HEAD, HEAD_W, HEAD_W), F32)],
        compiler_params=_cparams(("parallel", "arbitrary"), VMEM_LIMIT),
        name="gdn",
    )(qkv, z, ba, conv_w, arow, dtrow, gnorm)


def _lambda(lv, lam_init):
    s01 = jnp.sum(lv[0:1] * lv[1:2], axis=-1, keepdims=True)
    s23 = jnp.sum(lv[2:3] * lv[3:4], axis=-1, keepdims=True)
    return jnp.exp(s01) - jnp.exp(s23) + lam_init


def _attn_kernel(qt_ref, k_ref, vt_ref, lv_ref, dn_ref, o_ref,
                 q1_ref, q2_ref, m1, l1, a1, m2, l2, a2, *, tq, lam_init):
    qi = pl.program_id(2)
    qt = qt_ref[...]
    chan = lax.broadcasted_iota(jnp.int32, qt.shape, 0)
    q1_ref[...] = jnp.where(chan < HALF_W, qt, jnp.zeros_like(qt))
    q2_ref[...] = jnp.where(chan >= HALF_W, qt, jnp.zeros_like(qt))
    maps = ((q1_ref, m1, l1, a1), (q2_ref, m2, l2, a2))
    for _, m, l, a in maps:
        m[...] = jnp.full(m.shape, NEG_INF, F32)
        l[...] = jnp.zeros(l.shape, F32)
        a[...] = jnp.zeros(a.shape, F32)

    def tiles(ki, n_tile, masked):
        offs = [pl.multiple_of((ki + j) * tq, tq) for j in range(n_tile)]
        k16 = [k_ref[pl.ds(off, tq), :] for off in offs]
        vt16 = [vt_ref[:, pl.ds(off, tq)] for off in offs]
        sts = [[jnp.dot(k, qr[...], preferred_element_type=F32) for k in k16] for qr, _, _, _ in maps]
        olds = [(m[...], l[...], a[...]) for _, m, l, a in maps]
        for st_list, (m_old, l_old, a_old), (_, m, l, a) in zip(sts, olds, maps):
            if masked:
                kv = lax.broadcasted_iota(jnp.int32, st_list[0].shape, 0)
                qq = lax.broadcasted_iota(jnp.int32, st_list[0].shape, 1)
                st_list = [jnp.where(kv <= qq, st, NEG_INF) for st in st_list]
            m_new = m_old
            for st in st_list:
                m_new = jnp.maximum(m_new, jnp.max(st, axis=0, keepdims=True))
            alpha = jnp.exp2(m_old - m_new)
            l_new = alpha * l_old
            a_new = alpha * a_old
            for st, vt in zip(st_list, vt16):
                p = jnp.exp2(st - m_new)
                l_new = l_new + jnp.sum(p, axis=0, keepdims=True)
                a_new = a_new + jnp.dot(vt, p.astype(BF16), preferred_element_type=F32)
            l[...] = l_new
            a[...] = a_new
            m[...] = m_new

    def body(kp, carry):
        tiles(2 * kp, 2, False)
        return carry

    lax.fori_loop(0, qi // 2, body, 0)

    @pl.when(qi % 2 == 1)
    def _():
        tiles(qi - 1, 1, False)

    tiles(qi, 1, True)

    lam = _lambda(lv_ref[...], lam_init)
    o = a1[...] / l1[...] - lam * (a2[...] / l2[...])
    o = o * lax.rsqrt(jnp.mean(o * o, axis=0, keepdims=True) + EPS) * dn_ref[...] * (1.0 - lam_init)
    o_ref[...] = o.T.astype(o_ref.dtype)


def _attn(qt, k16, vt, lv, dnorm_col, n_seq, lam_init, tq_pref=512):
    m = k16.shape[0]
    t_len = m // n_seq
    tq = _row_tile(t_len, tq_pref)
    n_q = t_len // tq
    full = lambda a: pl.BlockSpec(a.shape, lambda b, h, i: (0, 0))
    stat = pltpu.VMEM((1, tq), F32)
    accs = pltpu.VMEM((HEAD_W, tq), F32)
    return pl.pallas_call(
        functools.partial(_attn_kernel, tq=tq, lam_init=lam_init),
        grid=(n_seq, N_HEAD, n_q),
        in_specs=[pl.BlockSpec((HEAD_W, tq), lambda b, h, i: (h, b * n_q + i)),
                  pl.BlockSpec((t_len, HEAD_W), lambda b, h, i: (b, h)),
                  pl.BlockSpec((HEAD_W, t_len), lambda b, h, i: (h, b)),
                  full(lv), full(dnorm_col)],
        out_specs=pl.BlockSpec((tq, HEAD_W), lambda b, h, i: (b * n_q + i, h)),
        out_shape=jax.ShapeDtypeStruct((m, BRANCH_W), BF16),
        scratch_shapes=[pltpu.VMEM((HEAD_W, tq), BF16), pltpu.VMEM((HEAD_W, tq), BF16),
                        stat, stat, accs, stat, stat, accs],
        compiler_params=_cparams(("parallel", "parallel", "arbitrary"), VMEM_LIMIT),
        name="diff_attn",
    )(qt, k16, vt, lv, dnorm_col)


def _ln_swish(y, g, b):
    yc = y - jnp.mean(y, axis=-1, keepdims=True)
    yn = yc * lax.rsqrt(jnp.mean(yc * yc, axis=-1, keepdims=True) + EPS) * g + b
    return _silu(yn)


CARRY_C = 32


def _conf_kernel(u_ref, w_ref, b_ref, lg_ref, lb_ref, o_ref, ext_ref, shift_ref, *, tm):
    t = pl.program_id(1)

    @pl.when(t == 0)
    def _():
        ext_ref[0:CARRY_C, :] = jnp.zeros((CARRY_C, ext_ref.shape[1]), F32)

    ext_ref[CARRY_C:CARRY_C + tm, :] = u_ref[...]
    base = CARRY_C - (CONV_C - 1)
    acc = None
    for phase in range(8):
        starts = [base + j for j in range(CONV_C) if (base + j) % 8 == phase]
        span = max(starts) - phase + tm
        shift_ref[0:span, :] = ext_ref[phase:phase + span, :]
        for st in starts:
            term = w_ref[st - base:st - base + 1, :] * shift_ref[st - phase:st - phase + tm, :]
            acc = term if acc is None else acc + term
    ext_ref[0:CARRY_C, :] = ext_ref[tm:tm + CARRY_C, :]
    o_ref[...] = _ln_swish(acc + b_ref[...], lg_ref[...], lb_ref[...]).astype(o_ref.dtype)


def _conf(u, w, b, lg, lb, n_seq, tm_pref=256):
    m, d = u.shape
    t_len = m // n_seq
    tm = _row_tile(t_len, tm_pref)
    n_t = t_len // tm
    row = pl.BlockSpec((tm, d), lambda s, t: (s * n_t + t, 0))
    full = lambda a: pl.BlockSpec(a.shape, lambda s, t: (0, 0))
    return pl.pallas_call(
        functools.partial(_conf_kernel, tm=tm),
        grid=(n_seq, n_t),
        in_specs=[row, full(w), full(b), full(lg), full(lb)],
        out_specs=row,
        out_shape=jax.ShapeDtypeStruct((m, d), BF16),
        scratch_shapes=[pltpu.VMEM((tm + CARRY_C, d), F32), pltpu.VMEM((tm + CARRY_C, d), F32)],
        compiler_params=_cparams(("parallel", "arbitrary")),
        name="conf_conv",
    )(u, w, b, lg, lb)


def _merge_kernel(x_ref, oa_ref, ob_ref, oc_ref, gates_ref, wbr_ref, wo_ref, out_ref):
    mix = None
    for b, o in enumerate((oa_ref, ob_ref, oc_ref)):
        pr = jnp.dot(o[...], wbr_ref[b], preferred_element_type=F32)
        term = gates_ref[:, b * D_MODEL:(b + 1) * D_MODEL] * pr
        mix = term if mix is None else mix + term
    out_ref[...] = x_ref[...] + jnp.dot(mix.astype(BF16), wo_ref[...], preferred_element_type=F32)


def _merge(x, oa, ob, oc, gates, wbr, wo, tm_pref=512):
    m, d = x.shape
    tm = _row_tile(m, tm_pref)
    row = lambda w: pl.BlockSpec((tm, w), lambda i: (i, 0))
    return pl.pallas_call(
        _merge_kernel,
        grid=(m // tm,),
        in_specs=[row(d), row(BRANCH_W), row(BRANCH_W), row(BRANCH_W), row(3 * d),
                  pl.BlockSpec(wbr.shape, lambda i: (0, 0, 0)), pl.BlockSpec(wo.shape, lambda i: (0, 0))],
        out_specs=row(d),
        out_shape=jax.ShapeDtypeStruct((m, d), F32),
        compiler_params=_cparams(("parallel",), VMEM_LIMIT),
        name="merge",
    )(x, oa, ob, oc, gates, wbr, wo)


def _final_norm_kernel(x_ref, g_ref, o_ref):
    o_ref[...] = _rms(x_ref[...], g_ref[...])


def _final_norm(x, g, tm_pref=512):
    m, d = x.shape
    tm = _row_tile(m, tm_pref)
    return pl.pallas_call(
        _final_norm_kernel,
        grid=(m // tm,),
        in_specs=[pl.BlockSpec((tm, d), lambda i: (i, 0)), pl.BlockSpec((1, d), lambda i: (0, 0))],
        out_specs=pl.BlockSpec((tm, d), lambda i: (i, 0)),
        out_shape=jax.ShapeDtypeStruct((m, d), F32),
        compiler_params=_cparams(("parallel",)),
        name="final_norm",
    )(x, g.reshape(1, d))


def _gdn_dec_kernel(qkv_ref, z_ref, ba_ref, st_ref, s0_ref, cw_ref, arow_ref, dtrow_ref, gn_ref,
                    o_ref, st_out_ref, s_out_ref, *, bt):
    qkv_w = N_HEAD * HEAD_W
    x = qkv_ref[...]
    y = cw_ref[CONV_A - 1:CONV_A, :] * x
    for j in range(CONV_A - 1):
        y = y + cw_ref[j:j + 1, :] * st_ref[j]
    for j in range(CONV_A - 2):
        st_out_ref[j] = st_ref[j + 1]
    st_out_ref[CONV_A - 2] = x
    qkv = _silu(y)
    beta, g_all = _gdn_gates(ba_ref[...], arow_ref[...], dtrow_ref[...])
    gn = gn_ref[...]
    pad = jnp.zeros((HEAD_W - bt, HEAD_W), F32)
    for hd in range(N_HEAD):
        qh = _l2norm(qkv[:, hd * HEAD_W:(hd + 1) * HEAD_W]) * (HEAD_W ** -0.5)
        kh = _l2norm(qkv[:, qkv_w + hd * HEAD_W:qkv_w + (hd + 1) * HEAD_W])
        vh = qkv[:, 2 * qkv_w + hd * HEAD_W:2 * qkv_w + (hd + 1) * HEAD_W]
        bcol = beta[:, hd:hd + 1]
        eg = jnp.exp(g_all[:, N_HEAD + hd:N_HEAD + hd + 1])
        qk = jnp.sum(qh * kh, axis=-1, keepdims=True)
        w_t = jnp.concatenate([kh * (bcol * eg), pad], axis=0).T
        q_t = jnp.concatenate([qh * eg, pad], axis=0).T
        k_t = jnp.concatenate([kh, pad], axis=0).T
        for b in range(bt):
            s = s0_ref[b, hd]
            ws = jnp.sum(w_t[:, b:b + 1] * s, axis=0, keepdims=True)
            qs = jnp.sum(q_t[:, b:b + 1] * s, axis=0, keepdims=True)
            v_new = vh[b:b + 1] * bcol[b:b + 1] - ws
            o = qs + qk[b:b + 1] * v_new
            s_out_ref[b, hd] = s * eg[b:b + 1] + k_t[:, b:b + 1] * v_new
            zc = z_ref[b:b + 1, hd * HEAD_W:(hd + 1) * HEAD_W]
            o_ref[b:b + 1, hd * HEAD_W:(hd + 1) * HEAD_W] = _rms(o, gn) * _silu(zc)


def _gdn_dec(qkv, z, ba, st, s0, conv_w, arow, dtrow, gnorm, bt=8):
    n, w = qkv.shape
    row = lambda wd: pl.BlockSpec((bt, wd), lambda i: (i, 0))
    full = lambda a: pl.BlockSpec(a.shape, lambda i: (0, 0))
    st_spec = pl.BlockSpec((CONV_A - 1, bt, w), lambda i: (0, i, 0))
    s_spec = pl.BlockSpec((bt, N_HEAD, HEAD_W, HEAD_W), lambda i: (i, 0, 0, 0))
    return pl.pallas_call(
        functools.partial(_gdn_dec_kernel, bt=bt),
        grid=(n // bt,),
        in_specs=[row(w), row(BRANCH_W), row(HEAD_W), st_spec, s_spec,
                  full(conv_w), full(arow), full(dtrow), full(gnorm)],
        out_specs=[row(BRANCH_W), st_spec, s_spec],
        out_shape=[jax.ShapeDtypeStruct((n, BRANCH_W), F32),
                   jax.ShapeDtypeStruct(st.shape, F32),
                   jax.ShapeDtypeStruct(s0.shape, F32)],
        compiler_params=_cparams(("parallel",)),
        name="gdn_decode",
    )(qkv, z, ba, st, s0, conv_w, arow, dtrow, gnorm)


def _attn_dec_kernel(pt_ref, q_ref, kn_ref, vn_ref, lv_ref, dn_ref, *rest, n_page, lam_init):
    k_refs = rest[:n_page]
    v_refs = rest[n_page:2 * n_page]
    o_ref, m_ref, l_ref, a_ref = rest[2 * n_page:]
    j = pl.program_id(1)

    @pl.when(j == 0)
    def _():
        m_ref[...] = jnp.full(m_ref.shape, NEG_INF, F32)
        l_ref[...] = jnp.zeros(l_ref.shape, F32)
        a_ref[...] = jnp.zeros(a_ref.shape, F32)

    q4 = q_ref[0]
    lane = lax.broadcasted_iota(jnp.int32, q4.shape, 1)
    qmat = jnp.concatenate([jnp.where(lane < HALF_W, q4, 0.0), jnp.where(lane >= HALF_W, q4, 0.0)], axis=0)
    qmat16 = qmat.astype(BF16)
    rows = PAGE * N_HEAD
    r = lax.broadcasted_iota(jnp.int32, (2 * N_HEAD, n_page * rows), 0)
    c = lax.broadcasted_iota(jnp.int32, (2 * N_HEAD, n_page * rows), 1)
    valid = (c % N_HEAD) == (r % N_HEAD)
    sc = jnp.concatenate(
        [lax.dot_general(qmat16, k_refs[p][...].astype(BF16), (((1,), (1,)), ((), ())),
                         preferred_element_type=F32) for p in range(n_page)], axis=1)
    sc = jnp.where(valid, sc, NEG_INF)
    m_old = m_ref[...]
    m_new = jnp.maximum(m_old, jnp.max(sc, axis=-1, keepdims=True))
    alpha = jnp.exp(m_old - m_new)
    pr = jnp.exp(sc - m_new)
    l_ref[...] = alpha * l_ref[...] + jnp.sum(pr, axis=-1, keepdims=True)
    pr = pr.astype(BF16)
    pv = None
    for p in range(n_page):
        term = jnp.dot(pr[:, p * rows:(p + 1) * rows], v_refs[p][...].astype(BF16), preferred_element_type=F32)
        pv = term if pv is None else pv + term
    a_ref[...] = alpha * a_ref[...] + pv
    m_ref[...] = m_new

    @pl.when(j == pl.num_programs(1) - 1)
    def _():
        k8 = jnp.concatenate([kn_ref[0], kn_ref[0]], axis=0)
        v8 = jnp.concatenate([vn_ref[0], vn_ref[0]], axis=0)
        s_self = jnp.sum(qmat * k8, axis=-1, keepdims=True)
        m_new = jnp.maximum(m_ref[...], s_self)
        alpha = jnp.exp(m_ref[...] - m_new)
        p_self = jnp.exp(s_self - m_new)
        l_fin = alpha * l_ref[...] + p_self
        a_fin = alpha * a_ref[...] + p_self * v8
        on = a_fin / l_fin
        lam = _lambda(lv_ref[...], lam_init)
        o = on[0:N_HEAD] - lam * on[N_HEAD:2 * N_HEAD]
        o_ref[0] = _rms(o, dn_ref[...]) * (1.0 - lam_init)


def _attn_dec(q, k_new, v_new, cache_k, cache_v, page_flat, layer, lv, dnorm, lam_init, n_page=8):
    n = q.shape[0]
    pages_per_seq = page_flat.shape[0] // n
    n_page = n_page if pages_per_seq % n_page == 0 else 1
    tok = pl.BlockSpec((1, N_HEAD, HEAD_W), lambda b, j, pt: (b, 0, 0))
    full = lambda a: pl.BlockSpec(a.shape, lambda b, j, pt: (0, 0))

    def page_spec(p):
        return pl.BlockSpec((None, None, PAGE * N_HEAD, HEAD_W),
                            lambda b, j, pt: (layer, pt[b * pages_per_seq + j * n_page + p], 0, 0))

    return pl.pallas_call(
        functools.partial(_attn_dec_kernel, n_page=n_page, lam_init=lam_init),
        grid_spec=pltpu.PrefetchScalarGridSpec(
            num_scalar_prefetch=1,
            grid=(n, pages_per_seq // n_page),
            in_specs=[tok, tok, tok, full(lv), full(dnorm)]
                     + [page_spec(p) for p in range(n_page)] * 2,
            out_specs=tok,
            scratch_shapes=[pltpu.VMEM((2 * N_HEAD, 1), F32), pltpu.VMEM((2 * N_HEAD, 1), F32),
                            pltpu.VMEM((2 * N_HEAD, HEAD_W), F32)],
        ),
        out_shape=jax.ShapeDtypeStruct((n, N_HEAD, HEAD_W), F32),
        compiler_params=_cparams(("parallel", "arbitrary")),
        name="diff_attn_decode",
    )(page_flat, q.reshape(n, N_HEAD, HEAD_W), k_new.reshape(n, N_HEAD, HEAD_W),
      v_new.reshape(n, N_HEAD, HEAD_W), lv, dnorm,
      *([cache_k] * n_page), *([cache_v] * n_page))


def _conf_dec_kernel(u_ref, st_ref, w_ref, b_ref, lg_ref, lb_ref, o_ref, st_out_ref):
    u = u_ref[...]
    acc = w_ref[CONV_C - 1:CONV_C, :] * u
    for j in range(CONV_C - 1):
        acc = acc + w_ref[j:j + 1, :] * st_ref[j]
    for j in range(CONV_C - 2):
        st_out_ref[j] = st_ref[j + 1]
    st_out_ref[CONV_C - 2] = u
    o_ref[...] = _ln_swish(acc + b_ref[...], lg_ref[...], lb_ref[...])


def _conf_dec(u, st, w, b, lg, lb):
    vm = pl.BlockSpec(memory_space=pltpu.VMEM)
    return pl.pallas_call(
        _conf_dec_kernel,
        in_specs=[vm] * 6,
        out_specs=[vm, vm],
        out_shape=[jax.ShapeDtypeStruct(u.shape, F32), jax.ShapeDtypeStruct(st.shape, F32)],
        name="conf_conv_decode",
    )(u, st, w, b, lg, lb)


def _layer_weights(l, p):
    qk_a = N_HEAD * HEAD_W
    sizes = (3 * qk_a, BRANCH_W, N_HEAD, N_HEAD, BRANCH_W, BRANCH_W, BRANCH_W, 2 * BRANCH_W, 3 * D_MODEL)
    offs = np.concatenate([[0], np.cumsum(sizes)])
    w_in = p["w_in"][l]
    col = lambda i: w_in[:, offs[i]:offs[i + 1]].astype(BF16)
    wba = jnp.pad(jnp.concatenate([w_in[:, offs[2]:offs[3]], w_in[:, offs[3]:offs[4]]], axis=1),
                  ((0, 0), (0, HEAD_W - 2 * N_HEAD))).astype(BF16)
    glu = col(7)
    lane_pad = lambda v: jnp.pad(v.reshape(1, N_HEAD), ((0, 0), (N_HEAD, HEAD_W - 2 * N_HEAD)))
    r2 = lambda v: v.reshape(1, -1)
    f1i, f1o = _ffn_weights(p["w_ffn1_in"][l], p["w_ffn1_out"][l])
    f2i, f2o = _ffn_weights(p["w_ffn2_in"][l], p["w_ffn2_out"][l])
    return dict(
        n1=p["norm_ffn1"][l], f1i=f1i, f1o=f1o,
        nm=r2(p["norm_mix"][l]),
        wqkv=col(0), wz=col(1), wba=wba, wq=col(4), wk=col(5), wv=col(6),
        wqt=col(4).T, wvt=col(6).T, dnc=p["diff_norm"][l].reshape(-1, 1),
        wga=glu[:, :BRANCH_W], wgb=glu[:, BRANCH_W:], wgate=col(8),
        cw=p["gdn_conv_w"][l], arow=lane_pad(p["gdn_a_log"][l]), dtrow=lane_pad(p["gdn_dt_bias"][l]),
        gn=r2(p["gdn_norm"][l]), lv=p["diff_lambda"][l], dn=r2(p["diff_norm"][l]),
        dw=p["conv_dw_w"][l], db=r2(p["conv_dw_b"][l]), lg=r2(p["conv_ln_g"][l]), lb=r2(p["conv_ln_b"][l]),
        wbr=p["w_branch"][l].astype(BF16), wo=p["w_out"][l].astype(BF16),
        n2=p["norm_ffn2"][l], f2i=f2i, f2o=f2o,
    )


def _prompt_layer(x, w, rope, n_seq, lam_init):
    t_len = x.shape[0] // n_seq
    x = _ffn(x, w["n1"], w["f1i"], w["f1o"])
    qkv, z, ba = _proj_gdn(x, w["nm"], w["wqkv"], w["wz"], w["wba"])
    qt, k, k16, v, vt = _proj_attn(x, w["nm"], w["wqt"], w["wk"], w["wv"], w["wvt"], *rope, t_len)
    u, gates = _proj_cg(x, w["nm"], w["wga"], w["wgb"], w["wgate"])
    o_a, s_new = _gdn(qkv, z, ba, w["cw"], w["arow"], w["dtrow"], w["gn"], n_seq)
    o_b = _attn(qt, k16, vt, w["lv"], w["dnc"], n_seq, lam_init)
    o_c = _conf(u, w["dw"], w["db"], w["lg"], w["lb"], n_seq)
    x = _merge(x, o_a, o_b, o_c, gates, w["wbr"], w["wo"])
    x = _ffn(x, w["n2"], w["f2i"], w["f2o"])
    gconv = qkv.reshape(n_seq, t_len, -1)[:, t_len - (CONV_A - 1):]
    cconv = u.reshape(n_seq, t_len, -1)[:, t_len - (CONV_C - 1):]
    return x, k, v, s_new, gconv, cconv


def _sample_layer(x, w, cos, sin, cache_k, cache_v, page_flat, layer, s0, gst, cst, lam_init):
    n = x.shape[0]
    x = _ffn(x, w["n1"], w["f1i"], w["f1o"])
    qkv, z, ba = _proj_gdn(x, w["nm"], w["wqkv"], w["wz"], w["wba"])
    q, k, v = _proj_attn_dec(x, w["nm"], w["wq"], w["wk"], w["wv"], cos, sin)
    u, gates = _proj_cg(x, w["nm"], w["wga"], w["wgb"], w["wgate"])
    o_a, gst_new, s_new = _gdn_dec(qkv, z, ba, jnp.swapaxes(gst, 0, 1), s0,
                                   w["cw"], w["arow"], w["dtrow"], w["gn"])
    o_b = _attn_dec(q, k, v, cache_k, cache_v, page_flat, layer, w["lv"], w["dn"], lam_init)
    o_c, cst_new = _conf_dec(u, jnp.swapaxes(cst, 0, 1), w["dw"], w["db"], w["lg"], w["lb"])
    x = _merge(x, o_a.astype(BF16), o_b.reshape(n, BRANCH_W).astype(BF16), o_c.astype(BF16),
               gates, w["wbr"], w["wo"])
    x = _ffn(x, w["n2"], w["f2i"], w["f2o"])
    return x, k, v, s_new, jnp.swapaxes(gst_new, 0, 1), jnp.swapaxes(cst_new, 0, 1)


def kernel(x_prompt, x_sample, cache_k, cache_v, page_table, state_gdn, state_gdn_conv, state_conv,
           norm_ffn1, w_ffn1_in, w_ffn1_out, norm_mix, w_in, gdn_conv_w, gdn_a_log, gdn_dt_bias,
           gdn_norm, diff_lambda, diff_norm, conv_dw_w, conv_dw_b, conv_ln_g, conv_ln_b,
           w_branch, w_out, norm_ffn2, w_ffn2_in, w_ffn2_out, norm_final):
    params = dict(norm_ffn1=norm_ffn1, w_ffn1_in=w_ffn1_in, w_ffn1_out=w_ffn1_out, norm_mix=norm_mix,
                  w_in=w_in, gdn_conv_w=gdn_conv_w, gdn_a_log=gdn_a_log, gdn_dt_bias=gdn_dt_bias,
                  gdn_norm=gdn_norm, diff_lambda=diff_lambda, diff_norm=diff_norm, conv_dw_w=conv_dw_w,
                  conv_dw_b=conv_dw_b, conv_ln_g=conv_ln_g, conv_ln_b=conv_ln_b, w_branch=w_branch,
                  w_out=w_out, norm_ffn2=norm_ffn2, w_ffn2_in=w_ffn2_in, w_ffn2_out=w_ffn2_out)
    depth = w_in.shape[0]
    bp, tp, d = x_prompt.shape
    bs, ts, _ = x_sample.shape
    past_len = page_table.shape[1] * PAGE
    n_pool = cache_k.shape[1]

    rope_p = _rope_table(0, tp)
    cos_s, sin_s, _, _ = _rope_table(past_len, 8)
    cos_s = jnp.broadcast_to(cos_s[0:1], (bs, HEAD_W))
    sin_s = jnp.broadcast_to(sin_s[0:1], (bs, HEAD_W))
    ck = cache_k.reshape(depth, n_pool, PAGE * N_HEAD, HEAD_W)
    cv = cache_v.reshape(depth, n_pool, PAGE * N_HEAD, HEAD_W)
    page_flat = page_table.reshape(-1)

    xp = x_prompt.reshape(bp * tp, d)
    xs = x_sample.reshape(bs * ts, d)
    outs = [[] for _ in range(10)]
    for l in range(depth):
        lam_init = 0.8 - 0.6 * math.exp(-0.3 * l)
        w = _layer_weights(l, params)
        xp, k_p, v_p, s_p, gc_p, c_p = _prompt_layer(xp, w, rope_p, bp, lam_init)
        xs, k_s, v_s, s_s, gc_s, c_s = _sample_layer(xs, w, cos_s, sin_s, ck, cv, page_flat, l,
                                                     state_gdn[l], state_gdn_conv[l], state_conv[l], lam_init)
        vals = (k_p.reshape(bp, tp, N_HEAD, HEAD_W), v_p.reshape(bp, tp, N_HEAD, HEAD_W),
                k_s.reshape(bs, ts, N_HEAD, HEAD_W), v_s.reshape(bs, ts, N_HEAD, HEAD_W),
                s_p, s_s, gc_p, gc_s, c_p, c_s)
        for lst, val in zip(outs, vals):
            lst.append(val)
    y_prompt = _final_norm(xp, norm_final).reshape(bp, tp, d)
    y_sample = _final_norm(xs, norm_final).reshape(bs, ts, d)
    return (y_prompt, y_sample) + tuple(jnp.stack(o) for o in outs)
```

```python
import functools
import math

import jax
import jax.numpy as jnp
import numpy as np
from jax import lax
from jax.experimental import pallas as pl
from jax.experimental.pallas import tpu as pltpu

F32 = jnp.float32
BF16 = jnp.bfloat16

D_MODEL = 1024
N_HEAD = 4
HEAD_W = 128
HALF_W = 64
BRANCH_W = N_HEAD * HEAD_W
CONV_A = 4
CHUNK = 64
CONV_C = 31
D_FF = 2816
EPS = 1e-6
ROPE_THETA = 10000.0
PAGE = 128
NEG_INF = float("-inf")

VMEM_LIMIT = 56 * 1024 * 1024


def _cparams(sem, vmem=None):
    return pltpu.CompilerParams(dimension_semantics=sem, vmem_limit_bytes=vmem)


def _rms(x, g):
    return x * lax.rsqrt(jnp.mean(x * x, axis=-1, keepdims=True) + EPS) * g


def _sigmoid(x):
    return 0.5 * jnp.tanh(0.5 * x) + 0.5


def _silu(x):
    return x * _sigmoid(x)


def _softplus(x):
    return jnp.maximum(x, 0.0) + jnp.log1p(jnp.exp(-jnp.abs(x)))


def _row_tile(m, pref):
    return pref if m % pref == 0 else m


FF_CHUNK = 256


def _ffn_kernel(x_ref, g_ref, wi_ref, wo_ref, o_ref):
    x = x_ref[...]
    h = _rms(x, g_ref[...]).astype(BF16)
    acc = None
    for c in range(D_FF // FF_CHUNK):
        cols = slice(c * FF_CHUNK, (c + 1) * FF_CHUNK)
        gate = jnp.dot(h, wi_ref[:, cols], preferred_element_type=F32)
        up = jnp.dot(h, wi_ref[:, D_FF + c * FF_CHUNK:D_FF + (c + 1) * FF_CHUNK], preferred_element_type=F32)
        act = (_silu(gate) * up).astype(BF16)
        term = jnp.dot(act, wo_ref[cols, :], preferred_element_type=F32)
        acc = term if acc is None else acc + term
    o_ref[...] = x + 0.5 * acc


def _ffn(x, g, w_in, w_out, tm_pref=512):
    m, d = x.shape
    tm = _row_tile(m, tm_pref)
    resident = lambda a: pl.BlockSpec(a.shape, lambda i: (0, 0), pipeline_mode=pl.Buffered(1))
    return pl.pallas_call(
        _ffn_kernel,
        grid=(m // tm,),
        in_specs=[pl.BlockSpec((tm, d), lambda i: (i, 0)), pl.BlockSpec((1, d), lambda i: (0, 0)),
                  resident(w_in), resident(w_out)],
        out_specs=pl.BlockSpec((tm, d), lambda i: (i, 0)),
        out_shape=jax.ShapeDtypeStruct((m, d), F32),
        compiler_params=_cparams(("parallel",), VMEM_LIMIT),
        name="ffn",
    )(x, g.reshape(1, d), w_in, w_out)


def _proj_gdn_kernel(x_ref, g_ref, wqkv_ref, wz_ref, wba_ref, qkv_ref, z_ref, ba_ref):
    h = _rms(x_ref[...], g_ref[...]).astype(BF16)
    qkv_ref[...] = jnp.dot(h, wqkv_ref[...], preferred_element_type=F32)
    z_ref[...] = jnp.dot(h, wz_ref[...], preferred_element_type=F32)
    ba_ref[...] = jnp.dot(h, wba_ref[...], preferred_element_type=F32)


def _proj_gdn(x, g, wqkv, wz, wba, tm_pref=512):
    m, d = x.shape
    tm = _row_tile(m, tm_pref)
    row = lambda w: pl.BlockSpec((tm, w), lambda i: (i, 0))
    full = lambda a: pl.BlockSpec(a.shape, lambda i: (0, 0))
    return pl.pallas_call(
        _proj_gdn_kernel,
        grid=(m // tm,),
        in_specs=[row(d), full(g), full(wqkv), full(wz), full(wba)],
        out_specs=[row(wqkv.shape[1]), row(wz.shape[1]), row(wba.shape[1])],
        out_shape=[jax.ShapeDtypeStruct((m, w.shape[1]), F32) for w in (wqkv, wz, wba)],
        compiler_params=_cparams(("parallel",), VMEM_LIMIT),
        name="proj_gdn",
    )(x, g, wqkv, wz, wba)


def _rope_table_kernel(inv_ref, invc_ref, cos_ref, sin_ref, cost_ref, sint_ref, *, pos0, tm):
    i = pl.program_id(0)
    half = HALF_W // 2
    pos = lax.broadcasted_iota(jnp.int32, (tm, HEAD_W), 0) + (i * tm + pos0)
    ang = pos.astype(F32) * inv_ref[...]
    first = (lax.broadcasted_iota(jnp.int32, (tm, HEAD_W), 1) % HALF_W) < half
    s = jnp.sin(ang)
    cos_ref[...] = jnp.cos(ang)
    sin_ref[...] = jnp.where(first, -s, s)
    pos_t = lax.broadcasted_iota(jnp.int32, (HEAD_W, tm), 1) + (i * tm + pos0)
    ang_t = pos_t.astype(F32) * invc_ref[...]
    first_t = (lax.broadcasted_iota(jnp.int32, (HEAD_W, tm), 0) % HALF_W) < half
    s_t = jnp.sin(ang_t)
    cost_ref[...] = jnp.cos(ang_t)
    sint_ref[...] = jnp.where(first_t, -s_t, s_t)


def _rope_table(pos0, n):
    half = HALF_W // 2
    inv = ROPE_THETA ** (-jnp.arange(half, dtype=F32) / half)
    inv_rep = jnp.tile(inv, HEAD_W // half)
    tm = _row_tile(n, 512)
    return pl.pallas_call(
        functools.partial(_rope_table_kernel, pos0=pos0, tm=tm),
        grid=(n // tm,),
        in_specs=[pl.BlockSpec((1, HEAD_W), lambda i: (0, 0)), pl.BlockSpec((HEAD_W, 1), lambda i: (0, 0))],
        out_specs=[pl.BlockSpec((tm, HEAD_W), lambda i: (i, 0))] * 2
                  + [pl.BlockSpec((HEAD_W, tm), lambda i: (0, i))] * 2,
        out_shape=[jax.ShapeDtypeStruct((n, HEAD_W), F32)] * 2 + [jax.ShapeDtypeStruct((HEAD_W, n), F32)] * 2,
        compiler_params=_cparams(("parallel",)),
        name="rope_table",
    )(inv_rep.reshape(1, HEAD_W), inv_rep.reshape(HEAD_W, 1))


def _rope_rows(p, cos, sin, scale, out_refs):
    half = HALF_W // 2
    first = (lax.broadcasted_iota(jnp.int32, cos.shape, 1) % HALF_W) < half
    for hd in range(N_HEAD):
        xh = p[:, hd * HEAD_W:(hd + 1) * HEAD_W]
        partner = jnp.where(first, pltpu.roll(xh, HEAD_W - half, 1), pltpu.roll(xh, half, 1))
        y = (xh * cos + partner * sin) * scale
        for o in out_refs:
            if len(o.shape) == 3:
                o[:, hd, :] = y.astype(o.dtype)
            else:
                o[:, hd * HEAD_W:(hd + 1) * HEAD_W] = y.astype(o.dtype)


def _proj_attn_dec_kernel(x_ref, g_ref, wq_ref, wk_ref, wv_ref, cos_ref, sin_ref, q_ref, k_ref, v_ref):
    h = _rms(x_ref[...], g_ref[...]).astype(BF16)
    cos = cos_ref[...]
    sin = sin_ref[...]
    _rope_rows(jnp.dot(h, wq_ref[...], preferred_element_type=F32), cos, sin, HALF_W ** -0.5, (q_ref,))
    _rope_rows(jnp.dot(h, wk_ref[...], preferred_element_type=F32), cos, sin, 1.0, (k_ref,))
    v_ref[...] = jnp.dot(h, wv_ref[...], preferred_element_type=F32)


def _proj_attn_dec(x, g, wq, wk, wv, cos, sin):
    vm = pl.BlockSpec(memory_space=pltpu.VMEM)
    m = x.shape[0]
    return pl.pallas_call(
        _proj_attn_dec_kernel,
        in_specs=[vm] * 7,
        out_specs=[vm] * 3,
        out_shape=[jax.ShapeDtypeStruct((m, BRANCH_W), F32)] * 3,
        compiler_params=pltpu.CompilerParams(vmem_limit_bytes=VMEM_LIMIT),
        name="proj_attn_decode",
    )(x, g, wq, wk, wv, cos, sin)


Q_SCALE_LOG2 = (HALF_W ** -0.5) * math.log2(math.e)


def _proj_attn_kernel(x_ref, g_ref, wqt_ref, wk_ref, wv_ref, wvt_ref, cos_ref, sin_ref, cost_ref, sint_ref,
                      *rest):
    qt_ref, k_ref, k16_ref, v_ref, vt_ref = rest[-5:]
    h = _rms(x_ref[...], g_ref[...]).astype(BF16)
    nt = (((1,), (1,)), ((), ()))
    _rope_rows(jnp.dot(h, wk_ref[...], preferred_element_type=F32), cos_ref[...], sin_ref[...], 1.0,
               (k_ref, k16_ref))
    v = jnp.dot(h, wv_ref[...], preferred_element_type=F32)
    for hd in range(N_HEAD):
        v_ref[:, hd, :] = v[:, hd * HEAD_W:(hd + 1) * HEAD_W]
    vt_ref[...] = lax.dot_general(wvt_ref[...], h, nt, preferred_element_type=F32).astype(vt_ref.dtype)
    qt = lax.dot_general(wqt_ref[...], h, nt, preferred_element_type=F32)
    cos_t = cost_ref[...]
    sin_t = sint_ref[...]
    half = HALF_W // 2
    for hd in range(N_HEAD):
        xh = qt[hd * HEAD_W:(hd + 1) * HEAD_W]
        partner = jnp.concatenate([xh[half:2 * half], xh[0:half], xh[3 * half:4 * half], xh[2 * half:3 * half]],
                                  axis=0)
        qt_ref[hd * HEAD_W:(hd + 1) * HEAD_W, :] = ((xh * cos_t + partner * sin_t) * Q_SCALE_LOG2).astype(qt_ref.dtype)


def _proj_attn(x, g, wqt, wk, wv, wvt, cos, sin, cos_t, sin_t, rows_per_seq, layer, depth, kv_bufs, tm_pref=512):
    m, d = x.shape
    tm = _row_tile(rows_per_seq, tm_pref)
    n_t = rows_per_seq // tm
    row = lambda w: pl.BlockSpec((tm, w), lambda i: (i, 0))
    row3 = pl.BlockSpec((None, tm, N_HEAD, HEAD_W), lambda i: (layer, i, 0, 0))
    colb = pl.BlockSpec((BRANCH_W, tm), lambda i: (0, i))
    full = lambda a: pl.BlockSpec(a.shape, lambda i: (0, 0))
    tab = pl.BlockSpec((tm, HEAD_W), lambda i: (i % n_t, 0))
    tab_t = pl.BlockSpec((HEAD_W, tm), lambda i: (0, i % n_t))
    in_specs = [row(d), full(g), full(wqt), full(wk), full(wv), full(wvt), tab, tab, tab_t, tab_t]
    args = [x, g, wqt, wk, wv, wvt, cos, sin, cos_t, sin_t]
    aliases = {}
    if kv_bufs is not None:
        aliases = {len(args): 1, len(args) + 1: 3}
        in_specs += [pl.BlockSpec(memory_space=pl.ANY)] * 2
        args += list(kv_bufs)
    buf = jax.ShapeDtypeStruct((depth, m, N_HEAD, HEAD_W), F32)
    return pl.pallas_call(
        _proj_attn_kernel,
        grid=(m // tm,),
        in_specs=in_specs,
        out_specs=[colb, row3, row(BRANCH_W), row3, colb],
        out_shape=[jax.ShapeDtypeStruct((BRANCH_W, m), BF16), buf,
                   jax.ShapeDtypeStruct((m, BRANCH_W), BF16), buf,
                   jax.ShapeDtypeStruct((BRANCH_W, m), BF16)],
        input_output_aliases=aliases,
        compiler_params=_cparams(("parallel",), VMEM_LIMIT),
        name="proj_attn",
    )(*args)


def _proj_cg_kernel(x_ref, g_ref, wa_ref, wb_ref, wgate_ref, u_ref, gates_ref):
    h = _rms(x_ref[...], g_ref[...]).astype(BF16)
    a = jnp.dot(h, wa_ref[...], preferred_element_type=F32)
    b = jnp.dot(h, wb_ref[...], preferred_element_type=F32)
    u_ref[...] = a * _sigmoid(b)
    gates_ref[...] = _sigmoid(jnp.dot(h, wgate_ref[...], preferred_element_type=F32)).astype(gates_ref.dtype)


def _proj_cg(x, g, wa, wb, wgate, tm_pref=512):
    m, d = x.shape
    tm = _row_tile(m, tm_pref)
    row = lambda w: pl.BlockSpec((tm, w), lambda i: (i, 0))
    full = lambda a: pl.BlockSpec(a.shape, lambda i: (0, 0))
    return pl.pallas_call(
        _proj_cg_kernel,
        grid=(m // tm,),
        in_specs=[row(d), full(g), full(wa), full(wb), full(wgate)],
        out_specs=[row(wa.shape[1]), row(wgate.shape[1])],
        out_shape=[jax.ShapeDtypeStruct((m, wa.shape[1]), F32),
                   jax.ShapeDtypeStruct((m, wgate.shape[1]), BF16)],
        compiler_params=_cparams(("parallel",), VMEM_LIMIT),
        name="proj_cg",
    )(x, g, wa, wb, wgate)


def _l2norm(x):
    return x * lax.rsqrt(jnp.sum(x * x, axis=-1, keepdims=True) + EPS)


def _split2(x):
    hi = x.astype(BF16)
    return hi, (x - hi.astype(F32)).astype(BF16)


def _split3(x):
    hi = x.astype(BF16)
    r1 = x - hi.astype(F32)
    mid = r1.astype(BF16)
    return hi, mid, (r1 - mid.astype(F32)).astype(BF16)


def _dot3(a, b):
    a_hi, a_lo = _split2(a)
    b_hi, b_lo = _split2(b)
    return (jnp.dot(a_hi, b_hi, preferred_element_type=F32) + jnp.dot(a_hi, b_lo, preferred_element_type=F32)
            + jnp.dot(a_lo, b_hi, preferred_element_type=F32))


def _gdn_gates(ba, arow, dtrow):
    beta = _sigmoid(ba)
    g = -jnp.exp(arow) * _softplus(ba + dtrow)
    return beta, g


def _gdn_kernel(qkv_ref, z_ref, ba_ref, cw_ref, arow_ref, dtrow_ref, gn_ref,
                o_ref, s_out_ref, ext_ref, s_ref, *, tm):
    t = pl.program_id(1)
    nc = tm // CHUNK
    qkv_w = N_HEAD * HEAD_W

    @pl.when(t == 0)
    def _():
        ext_ref[0:8, :] = jnp.zeros((8, ext_ref.shape[1]), F32)
        s_ref[...] = jnp.zeros(s_ref.shape, F32)

    ext_ref[8:8 + tm, :] = qkv_ref[...]
    y = cw_ref[CONV_A - 1:CONV_A, :] * ext_ref[8:8 + tm, :]
    for j in range(CONV_A - 1):
        off = 8 - (CONV_A - 1) + j
        y = y + cw_ref[j:j + 1, :] * ext_ref[off:off + tm, :]
    ext_ref[0:8, :] = ext_ref[tm:tm + 8, :]
    qkv = _silu(y)

    beta, g_all = _gdn_gates(ba_ref[...], arow_ref[...], dtrow_ref[...])
    r = lax.broadcasted_iota(jnp.int32, (tm, tm), 0)
    c = lax.broadcasted_iota(jnp.int32, (tm, tm), 1)
    same = r // CHUNK == c // CHUNK
    lbd = jnp.where(same, jnp.where(c <= r, 1.0, 0.0), 0.0).astype(BF16)
    gc = None
    for piece in _split3(g_all):
        term = jnp.dot(lbd, piece, preferred_element_type=F32)
        gc = term if gc is None else gc + term
    gc_t = gc.T

    ri = lax.broadcasted_iota(jnp.int32, (CHUNK, tm), 0)
    rc = lax.broadcasted_iota(jnp.int32, (CHUNK, tm), 1)
    eye_row = jnp.where(ri == rc % CHUNK, 1.0, 0.0)

    def to_bd(x_row):
        return jnp.where(same, jnp.concatenate([x_row] * nc, axis=0), 0.0)

    def from_bd(x_bd):
        acc = x_bd[0:CHUNK]
        for ch in range(1, nc):
            acc = acc + x_bd[ch * CHUNK:(ch + 1) * CHUNK]
        return acc

    gn = gn_ref[...]
    nt = (((1,), (1,)), ((), ()))
    heads = range(N_HEAD)
    qh = [_l2norm(qkv[:, hd * HEAD_W:(hd + 1) * HEAD_W]) * (HEAD_W ** -0.5) for hd in heads]
    kh = [_l2norm(qkv[:, qkv_w + hd * HEAD_W:qkv_w + (hd + 1) * HEAD_W]) for hd in heads]
    vh = [qkv[:, 2 * qkv_w + hd * HEAD_W:2 * qkv_w + (hd + 1) * HEAD_W] for hd in heads]
    gcol = [gc[:, N_HEAD + hd:N_HEAD + hd + 1] for hd in heads]
    grow = [gc_t[N_HEAD + hd:N_HEAD + hd + 1, :] for hd in heads]
    bcol = [beta[:, hd:hd + 1] for hd in heads]
    egc = [jnp.exp(g) for g in gcol]
    k16 = [k.astype(BF16) for k in kh]
    lower = jnp.logical_and(same, r >= c)
    decay = [jnp.exp(jnp.where(lower, gcol[hd] - grow[hd], NEG_INF)) for hd in heads]
    kk = [lax.dot_general(k16[hd], k16[hd], nt, preferred_element_type=F32) for hd in heads]
    qk = [lax.dot_general(qh[hd].astype(BF16), k16[hd], nt, preferred_element_type=F32) for hd in heads]
    n_bd = [jnp.where(r > c, -(kk[hd] * decay[hd] * bcol[hd]), 0.0) for hd in heads]
    attn16 = [(qk[hd] * decay[hd]).astype(BF16) for hd in heads]
    n_row = [from_bd(n) for n in n_bd]
    s_inv = [eye_row + n for n in n_row]
    x_row = [_dot3(n_row[hd], n_bd[hd]) for hd in heads]
    for _ in range(4):
        res = [_dot3(jnp.concatenate([s_inv[hd], x_row[hd]], axis=0), to_bd(x_row[hd])) for hd in heads]
        s_inv = [s_inv[hd] + res[hd][0:CHUNK] for hd in heads]
        x_row = [res[hd][CHUNK:2 * CHUNK] for hd in heads]
    s_inv = [s_inv[hd] + _dot3(s_inv[hd], to_bd(x_row[hd])) for hd in heads]
    sol = [_dot3(to_bd(s_inv[hd]), jnp.concatenate([vh[hd] * bcol[hd], kh[hd] * (bcol[hd] * egc[hd])], axis=1))
           for hd in heads]
    qg16 = [(qh[hd] * egc[hd]).astype(BF16) for hd in heads]
    w16 = [sol[hd][:, HEAD_W:].astype(BF16) for hd in heads]
    s = [s_ref[hd] for hd in heads]
    v_news = [[] for _ in heads]
    o_inters = [[] for _ in heads]
    for ch in range(nc):
        sl = slice(ch * CHUNK, (ch + 1) * CHUNK)
        for hd in heads:
            g_last = gcol[hd][ch * CHUNK + CHUNK - 1:(ch + 1) * CHUNK]
            res = jnp.dot(jnp.concatenate([w16[hd][sl], qg16[hd][sl]], axis=0), s[hd].astype(BF16),
                          preferred_element_type=F32)
            v_new = sol[hd][sl, :HEAD_W] - res[0:CHUNK]
            o_inters[hd].append(res[CHUNK:2 * CHUNK])
            v_news[hd].append(v_new)
            kd16 = (kh[hd][sl] * jnp.exp(g_last - gcol[hd][sl])).astype(BF16)
            s[hd] = s[hd] * jnp.exp(g_last) + lax.dot_general(kd16, v_new.astype(BF16), (((0,), (0,)), ((), ())),
                                                              preferred_element_type=F32)
    for hd in heads:
        s_ref[hd] = s[hd]
        o = jnp.concatenate(o_inters[hd], axis=0) + jnp.dot(
            attn16[hd], jnp.concatenate(v_news[hd], axis=0).astype(BF16), preferred_element_type=F32)
        zc = z_ref[:, hd * HEAD_W:(hd + 1) * HEAD_W]
        o_ref[:, hd * HEAD_W:(hd + 1) * HEAD_W] = (_rms(o, gn) * _silu(zc)).astype(o_ref.dtype)

    @pl.when(t == pl.num_programs(1) - 1)
    def _():
        s_out_ref[0] = s_ref[...]


def _gdn(qkv, z, ba, conv_w, arow, dtrow, gnorm, n_seq, tm_pref=256):
    m, w = qkv.shape
    t_len = m // n_seq
    tm = _row_tile(t_len, tm_pref)
    n_t = t_len // tm
    row = lambda wd: pl.BlockSpec((tm, wd), lambda b, t: (b * n_t + t, 0))
    full = lambda a: pl.BlockSpec(a.shape, lambda b, t: (0, 0))
    return pl.pallas_call(
        functools.partial(_gdn_kernel, tm=tm),
        grid=(n_seq, n_t),
        in_specs=[row(w), row(BRANCH_W), row(HEAD_W), full(conv_w), full(arow), full(dtrow), full(gnorm)],
        out_specs=[row(BRANCH_W),
                   pl.BlockSpec((1, N_HEAD, HEAD_W, HEAD_W), lambda b, t: (b, 0, 0, 0))],
        out_shape=[jax.ShapeDtypeStruct((m, BRANCH_W), BF16),
                   jax.ShapeDtypeStruct((n_seq, N_HEAD, HEAD_W, HEAD_W), F32)],
        scratch_shapes=[pltpu.VMEM((tm + 8, w), F32), pltpu.VMEM((N_HEAD, HEAD_W, HEAD_W), F32)],
        compiler_params=_cparams(("parallel", "arbitrary"), VMEM_LIMIT),
        name="gdn",
    )(qkv, z, ba, conv_w, arow, dtrow, gnorm)


def _lambda(lv, lam_init):
    s01 = jnp.sum(lv[0:1] * lv[1:2], axis=-1, keepdims=True)
    s23 = jnp.sum(lv[2:3] * lv[3:4], axis=-1, keepdims=True)
    return jnp.exp(s01) - jnp.exp(s23) + lam_init


KV_GROUP = 4


def _attn_kernel(qt_ref, k_ref, vt_ref, lv_ref, dn_ref, o_ref,
                 q1_ref, q2_ref, m1, l1, a1, m2, l2, a2, *, tq, lam_init):
    qi = pl.program_id(2)
    qt = qt_ref[...]
    chan = lax.broadcasted_iota(jnp.int32, qt.shape, 0)
    q1_ref[...] = jnp.where(chan < HALF_W, qt, jnp.zeros_like(qt))
    q2_ref[...] = jnp.where(chan >= HALF_W, qt, jnp.zeros_like(qt))
    maps = ((q1_ref, m1, l1, a1), (q2_ref, m2, l2, a2))
    for _, m, l, a in maps:
        m[...] = jnp.full(m.shape, NEG_INF, F32)
        l[...] = jnp.zeros(l.shape, F32)
        a[...] = jnp.zeros(a.shape, F32)

    def tiles(ki, n_tile, masked):
        offs = [pl.multiple_of((ki + j) * tq, tq) for j in range(n_tile)]
        k16 = [k_ref[pl.ds(off, tq), :] for off in offs]
        vt16 = [vt_ref[:, pl.ds(off, tq)] for off in offs]
        sts = [[jnp.dot(k, qr[...], preferred_element_type=F32) for k in k16] for qr, _, _, _ in maps]
        olds = [(m[...], l[...], a[...]) for _, m, l, a in maps]
        for st_list, (m_old, l_old, a_old), (_, m, l, a) in zip(sts, olds, maps):
            if masked:
                kv = lax.broadcasted_iota(jnp.int32, st_list[0].shape, 0)
                qq = lax.broadcasted_iota(jnp.int32, st_list[0].shape, 1)
                st_list = [jnp.where(kv <= qq, st, NEG_INF) for st in st_list]
            m_new = m_old
            for st in st_list:
                m_new = jnp.maximum(m_new, jnp.max(st, axis=0, keepdims=True))
            alpha = jnp.exp2(m_old - m_new)
            l_new = alpha * l_old
            a_new = alpha * a_old
            for st, vt in zip(st_list, vt16):
                p = jnp.exp2(st - m_new)
                l_new = l_new + jnp.sum(p, axis=0, keepdims=True)
                a_new = a_new + jnp.dot(vt, p.astype(BF16), preferred_element_type=F32)
            l[...] = l_new
            a[...] = a_new
            m[...] = m_new

    def body(kp, carry):
        tiles(KV_GROUP * kp, KV_GROUP, False)
        return carry

    def body_rest(ki, carry):
        tiles(ki, 1, False)
        return carry

    n_group = qi // KV_GROUP
    lax.fori_loop(0, n_group, body, 0)
    lax.fori_loop(n_group * KV_GROUP, qi, body_rest, 0)
    tiles(qi, 1, True)

    lam = _lambda(lv_ref[...], lam_init)
    o = a1[...] / l1[...] - lam * (a2[...] / l2[...])
    o = o * lax.rsqrt(jnp.mean(o * o, axis=0, keepdims=True) + EPS) * dn_ref[...] * (1.0 - lam_init)
    o_ref[...] = o.T.astype(o_ref.dtype)


def _attn(qt, k16, vt, lv, dnorm_col, n_seq, lam_init, tq_pref=512):
    m = k16.shape[0]
    t_len = m // n_seq
    tq = _row_tile(t_len, tq_pref)
    n_q = t_len // tq
    full = lambda a: pl.BlockSpec(a.shape, lambda b, h, i: (0, 0))
    stat = pltpu.VMEM((1, tq), F32)
    accs = pltpu.VMEM((HEAD_W, tq), F32)
    return pl.pallas_call(
        functools.partial(_attn_kernel, tq=tq, lam_init=lam_init),
        grid=(n_seq, N_HEAD, n_q),
        in_specs=[pl.BlockSpec((HEAD_W, tq), lambda b, h, i: (h, b * n_q + i)),
                  pl.BlockSpec((t_len, HEAD_W), lambda b, h, i: (b, h)),
                  pl.BlockSpec((HEAD_W, t_len), lambda b, h, i: (h, b)),
                  full(lv), full(dnorm_col)],
        out_specs=pl.BlockSpec((tq, HEAD_W), lambda b, h, i: (b * n_q + i, h)),
        out_shape=jax.ShapeDtypeStruct((m, BRANCH_W), BF16),
        scratch_shapes=[pltpu.VMEM((HEAD_W, tq), BF16), pltpu.VMEM((HEAD_W, tq), BF16),
                        stat, stat, accs, stat, stat, accs],
        compiler_params=_cparams(("parallel", "parallel", "arbitrary"), VMEM_LIMIT),
        name="diff_attn",
    )(qt, k16, vt, lv, dnorm_col)


def _ln_swish(y, g, b):
    yc = y - jnp.mean(y, axis=-1, keepdims=True)
    yn = yc * lax.rsqrt(jnp.mean(yc * yc, axis=-1, keepdims=True) + EPS) * g + b
    return _silu(yn)


CARRY_C = 32


def _conf_kernel(u_ref, w_ref, b_ref, lg_ref, lb_ref, o_ref, ext_ref, shift_ref, *, tm):
    t = pl.program_id(1)

    @pl.when(t == 0)
    def _():
        ext_ref[0:CARRY_C, :] = jnp.zeros((CARRY_C, ext_ref.shape[1]), F32)

    ext_ref[CARRY_C:CARRY_C + tm, :] = u_ref[...]
    base = CARRY_C - (CONV_C - 1)
    acc = None
    for phase in range(8):
        starts = [base + j for j in range(CONV_C) if (base + j) % 8 == phase]
        span = max(starts) - phase + tm
        shift_ref[0:span, :] = ext_ref[phase:phase + span, :]
        for st in starts:
            term = w_ref[st - base:st - base + 1, :] * shift_ref[st - phase:st - phase + tm, :]
            acc = term if acc is None else acc + term
    ext_ref[0:CARRY_C, :] = ext_ref[tm:tm + CARRY_C, :]
    o_ref[...] = _ln_swish(acc + b_ref[...], lg_ref[...], lb_ref[...]).astype(o_ref.dtype)


def _conf(u, w, b, lg, lb, n_seq, tm_pref=256):
    m, d = u.shape
    t_len = m // n_seq
    tm = _row_tile(t_len, tm_pref)
    n_t = t_len // tm
    row = pl.BlockSpec((tm, d), lambda s, t: (s * n_t + t, 0))
    full = lambda a: pl.BlockSpec(a.shape, lambda s, t: (0, 0))
    return pl.pallas_call(
        functools.partial(_conf_kernel, tm=tm),
        grid=(n_seq, n_t),
        in_specs=[row, full(w), full(b), full(lg), full(lb)],
        out_specs=row,
        out_shape=jax.ShapeDtypeStruct((m, d), BF16),
        scratch_shapes=[pltpu.VMEM((tm + CARRY_C, d), F32), pltpu.VMEM((tm + CARRY_C, d), F32)],
        compiler_params=_cparams(("parallel", "arbitrary")),
        name="conf_conv",
    )(u, w, b, lg, lb)


def _merge_kernel(x_ref, oa_ref, ob_ref, oc_ref, gates_ref, wbr_ref, wo_ref, out_ref):
    mix = None
    for b, o in enumerate((oa_ref, ob_ref, oc_ref)):
        pr = jnp.dot(o[...], wbr_ref[b], preferred_element_type=F32)
        term = gates_ref[:, b * D_MODEL:(b + 1) * D_MODEL] * pr
        mix = term if mix is None else mix + term
    out_ref[...] = x_ref[...] + jnp.dot(mix.astype(BF16), wo_ref[...], preferred_element_type=F32)


def _merge(x, oa, ob, oc, gates, wbr, wo, tm_pref=512):
    m, d = x.shape
    tm = _row_tile(m, tm_pref)
    row = lambda w: pl.BlockSpec((tm, w), lambda i: (i, 0))
    return pl.pallas_call(
        _merge_kernel,
        grid=(m // tm,),
        in_specs=[row(d), row(BRANCH_W), row(BRANCH_W), row(BRANCH_W), row(3 * d),
                  pl.BlockSpec(wbr.shape, lambda i: (0, 0, 0)), pl.BlockSpec(wo.shape, lambda i: (0, 0))],
        out_specs=row(d),
        out_shape=jax.ShapeDtypeStruct((m, d), F32),
        compiler_params=_cparams(("parallel",), VMEM_LIMIT),
        name="merge",
    )(x, oa, ob, oc, gates, wbr, wo)


def _final_norm_kernel(x_ref, g_ref, o_ref):
    o_ref[...] = _rms(x_ref[...], g_ref[...])


def _final_norm(x, g, tm_pref=512):
    m, d = x.shape
    tm = _row_tile(m, tm_pref)
    return pl.pallas_call(
        _final_norm_kernel,
        grid=(m // tm,),
        in_specs=[pl.BlockSpec((tm, d), lambda i: (i, 0)), pl.BlockSpec((1, d), lambda i: (0, 0))],
        out_specs=pl.BlockSpec((tm, d), lambda i: (i, 0)),
        out_shape=jax.ShapeDtypeStruct((m, d), F32),
        compiler_params=_cparams(("parallel",)),
        name="final_norm",
    )(x, g.reshape(1, d))


def _gdn_dec_kernel(qkv_ref, z_ref, ba_ref, st_ref, s0_ref, cw_ref, arow_ref, dtrow_ref, gn_ref,
                    o_ref, st_out_ref, s_out_ref, *, bt):
    qkv_w = N_HEAD * HEAD_W
    x = qkv_ref[...]
    y = cw_ref[CONV_A - 1:CONV_A, :] * x
    for j in range(CONV_A - 1):
        y = y + cw_ref[j:j + 1, :] * st_ref[j]
    for j in range(CONV_A - 2):
        st_out_ref[j] = st_ref[j + 1]
    st_out_ref[CONV_A - 2] = x
    qkv = _silu(y)
    beta, g_all = _gdn_gates(ba_ref[...], arow_ref[...], dtrow_ref[...])
    gn = gn_ref[...]
    pad = jnp.zeros((HEAD_W - bt, HEAD_W), F32)
    for hd in range(N_HEAD):
        qh = _l2norm(qkv[:, hd * HEAD_W:(hd + 1) * HEAD_W]) * (HEAD_W ** -0.5)
        kh = _l2norm(qkv[:, qkv_w + hd * HEAD_W:qkv_w + (hd + 1) * HEAD_W])
        vh = qkv[:, 2 * qkv_w + hd * HEAD_W:2 * qkv_w + (hd + 1) * HEAD_W]
        bcol = beta[:, hd:hd + 1]
        eg = jnp.exp(g_all[:, N_HEAD + hd:N_HEAD + hd + 1])
        qk = jnp.sum(qh * kh, axis=-1, keepdims=True)
        w_t = jnp.concatenate([kh * (bcol * eg), pad], axis=0).T
        q_t = jnp.concatenate([qh * eg, pad], axis=0).T
        k_t = jnp.concatenate([kh, pad], axis=0).T
        for b in range(bt):
            s = s0_ref[b, hd]
            ws = jnp.sum(w_t[:, b:b + 1] * s, axis=0, keepdims=True)
            qs = jnp.sum(q_t[:, b:b + 1] * s, axis=0, keepdims=True)
            v_new = vh[b:b + 1] * bcol[b:b + 1] - ws
            o = qs + qk[b:b + 1] * v_new
            s_out_ref[b, hd] = s * eg[b:b + 1] + k_t[:, b:b + 1] * v_new
            zc = z_ref[b:b + 1, hd * HEAD_W:(hd + 1) * HEAD_W]
            o_ref[b:b + 1, hd * HEAD_W:(hd + 1) * HEAD_W] = _rms(o, gn) * _silu(zc)


def _gdn_dec(qkv, z, ba, st, s0, conv_w, arow, dtrow, gnorm, bt=8):
    n, w = qkv.shape
    row = lambda wd: pl.BlockSpec((bt, wd), lambda i: (i, 0))
    full = lambda a: pl.BlockSpec(a.shape, lambda i: (0, 0))
    st_spec = pl.BlockSpec((CONV_A - 1, bt, w), lambda i: (0, i, 0))
    s_spec = pl.BlockSpec((bt, N_HEAD, HEAD_W, HEAD_W), lambda i: (i, 0, 0, 0))
    return pl.pallas_call(
        functools.partial(_gdn_dec_kernel, bt=bt),
        grid=(n // bt,),
        in_specs=[row(w), row(BRANCH_W), row(HEAD_W), st_spec, s_spec,
                  full(conv_w), full(arow), full(dtrow), full(gnorm)],
        out_specs=[row(BRANCH_W), st_spec, s_spec],
        out_shape=[jax.ShapeDtypeStruct((n, BRANCH_W), F32),
                   jax.ShapeDtypeStruct(st.shape, F32),
                   jax.ShapeDtypeStruct(s0.shape, F32)],
        compiler_params=_cparams(("parallel",)),
        name="gdn_decode",
    )(qkv, z, ba, st, s0, conv_w, arow, dtrow, gnorm)


def _attn_dec_kernel(pt_ref, q_ref, kn_ref, vn_ref, lv_ref, dn_ref, *rest, n_page, lam_init):
    k_refs = rest[:n_page]
    v_refs = rest[n_page:2 * n_page]
    o_ref, m_ref, l_ref, a_ref = rest[2 * n_page:]
    j = pl.program_id(1)

    @pl.when(j == 0)
    def _():
        m_ref[...] = jnp.full(m_ref.shape, NEG_INF, F32)
        l_ref[...] = jnp.zeros(l_ref.shape, F32)
        a_ref[...] = jnp.zeros(a_ref.shape, F32)

    q4 = q_ref[0]
    lane = lax.broadcasted_iota(jnp.int32, q4.shape, 1)
    qmat = jnp.concatenate([jnp.where(lane < HALF_W, q4, 0.0), jnp.where(lane >= HALF_W, q4, 0.0)], axis=0)
    qmat16 = qmat.astype(BF16)
    rows = PAGE * N_HEAD
    r = lax.broadcasted_iota(jnp.int32, (2 * N_HEAD, n_page * rows), 0)
    c = lax.broadcasted_iota(jnp.int32, (2 * N_HEAD, n_page * rows), 1)
    valid = (c % N_HEAD) == (r % N_HEAD)
    sc = jnp.concatenate(
        [lax.dot_general(qmat16, k_refs[p][...].astype(BF16), (((1,), (1,)), ((), ())),
                         preferred_element_type=F32) for p in range(n_page)], axis=1)
    sc = jnp.where(valid, sc, NEG_INF)
    m_old = m_ref[...]
    m_new = jnp.maximum(m_old, jnp.max(sc, axis=-1, keepdims=True))
    alpha = jnp.exp(m_old - m_new)
    pr = jnp.exp(sc - m_new)
    l_ref[...] = alpha * l_ref[...] + jnp.sum(pr, axis=-1, keepdims=True)
    pr = pr.astype(BF16)
    pv = None
    for p in range(n_page):
        term = jnp.dot(pr[:, p * rows:(p + 1) * rows], v_refs[p][...].astype(BF16), preferred_element_type=F32)
        pv = term if pv is None else pv + term
    a_ref[...] = alpha * a_ref[...] + pv
    m_ref[...] = m_new

    @pl.when(j == pl.num_programs(1) - 1)
    def _():
        k8 = jnp.concatenate([kn_ref[0], kn_ref[0]], axis=0)
        v8 = jnp.concatenate([vn_ref[0], vn_ref[0]], axis=0)
        s_self = jnp.sum(qmat * k8, axis=-1, keepdims=True)
        m_new = jnp.maximum(m_ref[...], s_self)
        alpha = jnp.exp(m_ref[...] - m_new)
        p_self = jnp.exp(s_self - m_new)
        l_fin = alpha * l_ref[...] + p_self
        a_fin = alpha * a_ref[...] + p_self * v8
        on = a_fin / l_fin
        lam = _lambda(lv_ref[...], lam_init)
        o = on[0:N_HEAD] - lam * on[N_HEAD:2 * N_HEAD]
        o_ref[0] = _rms(o, dn_ref[...]) * (1.0 - lam_init)


def _attn_dec(q, k_new, v_new, cache_k, cache_v, page_flat, layer, lv, dnorm, lam_init, n_page=16):
    n = q.shape[0]
    pages_per_seq = page_flat.shape[0] // n
    n_page = n_page if pages_per_seq % n_page == 0 else 1
    tok = pl.BlockSpec((1, N_HEAD, HEAD_W), lambda b, j, pt: (b, 0, 0))
    full = lambda a: pl.BlockSpec(a.shape, lambda b, j, pt: (0, 0))

    def page_spec(p):
        return pl.BlockSpec((None, None, PAGE * N_HEAD, HEAD_W),
                            lambda b, j, pt: (layer, pt[b * pages_per_seq + j * n_page + p], 0, 0))

    return pl.pallas_call(
        functools.partial(_attn_dec_kernel, n_page=n_page, lam_init=lam_init),
        grid_spec=pltpu.PrefetchScalarGridSpec(
            num_scalar_prefetch=1,
            grid=(n, pages_per_seq // n_page),
            in_specs=[tok, tok, tok, full(lv), full(dnorm)]
                     + [page_spec(p) for p in range(n_page)] * 2,
            out_specs=tok,
            scratch_shapes=[pltpu.VMEM((2 * N_HEAD, 1), F32), pltpu.VMEM((2 * N_HEAD, 1), F32),
                            pltpu.VMEM((2 * N_HEAD, HEAD_W), F32)],
        ),
        out_shape=jax.ShapeDtypeStruct((n, N_HEAD, HEAD_W), F32),
        compiler_params=_cparams(("parallel", "arbitrary")),
        name="diff_attn_decode",
    )(page_flat, q.reshape(n, N_HEAD, HEAD_W), k_new.reshape(n, N_HEAD, HEAD_W),
      v_new.reshape(n, N_HEAD, HEAD_W), lv, dnorm,
      *([cache_k] * n_page), *([cache_v] * n_page))


def _conf_dec_kernel(u_ref, st_ref, w_ref, b_ref, lg_ref, lb_ref, o_ref, st_out_ref):
    u = u_ref[...]
    acc = w_ref[CONV_C - 1:CONV_C, :] * u
    for j in range(CONV_C - 1):
        acc = acc + w_ref[j:j + 1, :] * st_ref[j]
    for j in range(CONV_C - 2):
        st_out_ref[j] = st_ref[j + 1]
    st_out_ref[CONV_C - 2] = u
    o_ref[...] = _ln_swish(acc + b_ref[...], lg_ref[...], lb_ref[...])


def _conf_dec(u, st, w, b, lg, lb):
    vm = pl.BlockSpec(memory_space=pltpu.VMEM)
    return pl.pallas_call(
        _conf_dec_kernel,
        in_specs=[vm] * 6,
        out_specs=[vm, vm],
        out_shape=[jax.ShapeDtypeStruct(u.shape, F32), jax.ShapeDtypeStruct(st.shape, F32)],
        name="conf_conv_decode",
    )(u, st, w, b, lg, lb)


def _layer_weights(l, p):
    qk_a = N_HEAD * HEAD_W
    sizes = (3 * qk_a, BRANCH_W, N_HEAD, N_HEAD, BRANCH_W, BRANCH_W, BRANCH_W, 2 * BRANCH_W, 3 * D_MODEL)
    offs = np.concatenate([[0], np.cumsum(sizes)])
    w_in = p["w_in"][l]
    col = lambda i: w_in[:, offs[i]:offs[i + 1]].astype(BF16)
    wba = jnp.pad(jnp.concatenate([w_in[:, offs[2]:offs[3]], w_in[:, offs[3]:offs[4]]], axis=1),
                  ((0, 0), (0, HEAD_W - 2 * N_HEAD))).astype(BF16)
    glu = col(7)
    lane_pad = lambda v: jnp.pad(v.reshape(1, N_HEAD), ((0, 0), (N_HEAD, HEAD_W - 2 * N_HEAD)))
    r2 = lambda v: v.reshape(1, -1)
    f1i, f1o = p["w_ffn1_in"][l].astype(BF16), p["w_ffn1_out"][l].astype(BF16)
    f2i, f2o = p["w_ffn2_in"][l].astype(BF16), p["w_ffn2_out"][l].astype(BF16)
    return dict(
        n1=p["norm_ffn1"][l], f1i=f1i, f1o=f1o,
        nm=r2(p["norm_mix"][l]),
        wqkv=col(0), wz=col(1), wba=wba, wq=col(4), wk=col(5), wv=col(6),
        wqt=col(4).T, wvt=col(6).T, dnc=p["diff_norm"][l].reshape(-1, 1),
        wga=glu[:, :BRANCH_W], wgb=glu[:, BRANCH_W:], wgate=col(8),
        cw=p["gdn_conv_w"][l], arow=lane_pad(p["gdn_a_log"][l]), dtrow=lane_pad(p["gdn_dt_bias"][l]),
        gn=r2(p["gdn_norm"][l]), lv=p["diff_lambda"][l], dn=r2(p["diff_norm"][l]),
        dw=p["conv_dw_w"][l], db=r2(p["conv_dw_b"][l]), lg=r2(p["conv_ln_g"][l]), lb=r2(p["conv_ln_b"][l]),
        wbr=p["w_branch"][l].astype(BF16), wo=p["w_out"][l].astype(BF16),
        n2=p["norm_ffn2"][l], f2i=f2i, f2o=f2o,
    )


def _prompt_layer(x, w, rope, n_seq, lam_init, layer, depth, kv_bufs):
    t_len = x.shape[0] // n_seq
    x = _ffn(x, w["n1"], w["f1i"], w["f1o"])
    qkv, z, ba = _proj_gdn(x, w["nm"], w["wqkv"], w["wz"], w["wba"])
    qt, k, k16, v, vt = _proj_attn(x, w["nm"], w["wqt"], w["wk"], w["wv"], w["wvt"], *rope, t_len,
                                   layer, depth, kv_bufs)
    u, gates = _proj_cg(x, w["nm"], w["wga"], w["wgb"], w["wgate"])
    o_a, s_new = _gdn(qkv, z, ba, w["cw"], w["arow"], w["dtrow"], w["gn"], n_seq)
    o_b = _attn(qt, k16, vt, w["lv"], w["dnc"], n_seq, lam_init)
    o_c = _conf(u, w["dw"], w["db"], w["lg"], w["lb"], n_seq)
    x = _merge(x, o_a, o_b, o_c, gates, w["wbr"], w["wo"])
    x = _ffn(x, w["n2"], w["f2i"], w["f2o"])
    gconv = qkv.reshape(n_seq, t_len, -1)[:, t_len - (CONV_A - 1):]
    cconv = u.reshape(n_seq, t_len, -1)[:, t_len - (CONV_C - 1):]
    return x, (k, v), s_new, gconv, cconv


def _sample_layer(x, w, cos, sin, cache_k, cache_v, page_flat, layer, s0, gst, cst, lam_init):
    n = x.shape[0]
    x = _ffn(x, w["n1"], w["f1i"], w["f1o"])
    qkv, z, ba = _proj_gdn(x, w["nm"], w["wqkv"], w["wz"], w["wba"])
    q, k, v = _proj_attn_dec(x, w["nm"], w["wq"], w["wk"], w["wv"], cos, sin)
    u, gates = _proj_cg(x, w["nm"], w["wga"], w["wgb"], w["wgate"])
    o_a, gst_new, s_new = _gdn_dec(qkv, z, ba, jnp.swapaxes(gst, 0, 1), s0,
                                   w["cw"], w["arow"], w["dtrow"], w["gn"])
    o_b = _attn_dec(q, k, v, cache_k, cache_v, page_flat, layer, w["lv"], w["dn"], lam_init)
    o_c, cst_new = _conf_dec(u, jnp.swapaxes(cst, 0, 1), w["dw"], w["db"], w["lg"], w["lb"])
    x = _merge(x, o_a.astype(BF16), o_b.reshape(n, BRANCH_W).astype(BF16), o_c.astype(BF16),
               gates, w["wbr"], w["wo"])
    x = _ffn(x, w["n2"], w["f2i"], w["f2o"])
    return x, k, v, s_new, jnp.swapaxes(gst_new, 0, 1), jnp.swapaxes(cst_new, 0, 1)


def kernel(x_prompt, x_sample, cache_k, cache_v, page_table, state_gdn, state_gdn_conv, state_conv,
           norm_ffn1, w_ffn1_in, w_ffn1_out, norm_mix, w_in, gdn_conv_w, gdn_a_log, gdn_dt_bias,
           gdn_norm, diff_lambda, diff_norm, conv_dw_w, conv_dw_b, conv_ln_g, conv_ln_b,
           w_branch, w_out, norm_ffn2, w_ffn2_in, w_ffn2_out, norm_final):
    params = dict(norm_ffn1=norm_ffn1, w_ffn1_in=w_ffn1_in, w_ffn1_out=w_ffn1_out, norm_mix=norm_mix,
                  w_in=w_in, gdn_conv_w=gdn_conv_w, gdn_a_log=gdn_a_log, gdn_dt_bias=gdn_dt_bias,
                  gdn_norm=gdn_norm, diff_lambda=diff_lambda, diff_norm=diff_norm, conv_dw_w=conv_dw_w,
                  conv_dw_b=conv_dw_b, conv_ln_g=conv_ln_g, conv_ln_b=conv_ln_b, w_branch=w_branch,
                  w_out=w_out, norm_ffn2=norm_ffn2, w_ffn2_in=w_ffn2_in, w_ffn2_out=w_ffn2_out)
    depth = w_in.shape[0]
    bp, tp, d = x_prompt.shape
    bs, ts, _ = x_sample.shape
    past_len = page_table.shape[1] * PAGE
    n_pool = cache_k.shape[1]

    rope_p = _rope_table(0, tp)
    cos_s, sin_s, _, _ = _rope_table(past_len, 8)
    cos_s = jnp.broadcast_to(cos_s[0:1], (bs, HEAD_W))
    sin_s = jnp.broadcast_to(sin_s[0:1], (bs, HEAD_W))
    ck = cache_k.reshape(depth, n_pool, PAGE * N_HEAD, HEAD_W)
    cv = cache_v.reshape(depth, n_pool, PAGE * N_HEAD, HEAD_W)
    page_flat = page_table.reshape(-1)

    xp = x_prompt.reshape(bp * tp, d)
    xs = x_sample.reshape(bs * ts, d)
    outs = [[] for _ in range(8)]
    kv_p = None
    for l in range(depth):
        lam_init = 0.8 - 0.6 * math.exp(-0.3 * l)
        w = _layer_weights(l, params)
        xp, kv_p, s_p, gc_p, c_p = _prompt_layer(xp, w, rope_p, bp, lam_init, l, depth, kv_p)
        xs, k_s, v_s, s_s, gc_s, c_s = _sample_layer(xs, w, cos_s, sin_s, ck, cv, page_flat, l,
                                                     state_gdn[l], state_gdn_conv[l], state_conv[l], lam_init)
        vals = (k_s.reshape(bs, ts, N_HEAD, HEAD_W), v_s.reshape(bs, ts, N_HEAD, HEAD_W),
                s_p, s_s, gc_p, gc_s, c_p, c_s)
        for lst, val in zip(outs, vals):
            lst.append(val)
    y_prompt = _final_norm(xp, norm_final).reshape(bp, tp, d)
    y_sample = _final_norm(xs, norm_final).reshape(bs, ts, d)
    k_prompt, v_prompt = (a.reshape(depth, bp, tp, N_HEAD, HEAD_W) for a in kv_p)
    return (y_prompt, y_sample, k_prompt, v_prompt) + tuple(jnp.stack(o) for o in outs)
```

```python
import functools
import math

import jax
import jax.numpy as jnp
import numpy as np
from jax import lax
from jax.experimental import pallas as pl
from jax.experimental.pallas import tpu as pltpu

F32 = jnp.float32
BF16 = jnp.bfloat16

D_MODEL = 1024
N_HEAD = 4
HEAD_W = 128
HALF_W = 64
BRANCH_W = N_HEAD * HEAD_W
CONV_A = 4
CHUNK = 64
CONV_C = 31
D_FF = 2816
EPS = 1e-6
ROPE_THETA = 10000.0
PAGE = 128
NEG_INF = float("-inf")

VMEM_LIMIT = 56 * 1024 * 1024


def _cparams(sem, vmem=None):
    return pltpu.CompilerParams(dimension_semantics=sem, vmem_limit_bytes=vmem)


def _rms(x, g):
    return x * lax.rsqrt(jnp.mean(x * x, axis=-1, keepdims=True) + EPS) * g


def _sigmoid(x):
    return 0.5 * jnp.tanh(0.5 * x) + 0.5


def _silu(x):
    return x * _sigmoid(x)


def _softplus(x):
    return jnp.maximum(x, 0.0) + jnp.log1p(jnp.exp(-jnp.abs(x)))


def _row_tile(m, pref):
    return pref if m % pref == 0 else m


FF_CHUNK = 256


def _ffn_body(x, g, wi_ref, wo_ref):
    h = _rms(x, g).astype(BF16)
    acc = None
    for c in range(D_FF // FF_CHUNK):
        cols = slice(c * FF_CHUNK, (c + 1) * FF_CHUNK)
        gate = jnp.dot(h, wi_ref[:, cols], preferred_element_type=F32)
        up = jnp.dot(h, wi_ref[:, D_FF + c * FF_CHUNK:D_FF + (c + 1) * FF_CHUNK], preferred_element_type=F32)
        act = (_silu(gate) * up).astype(BF16)
        term = jnp.dot(act, wo_ref[cols, :], preferred_element_type=F32)
        acc = term if acc is None else acc + term
    return x + 0.5 * acc


def _ffn_kernel(x_ref, g_ref, wi_ref, wo_ref, o_ref):
    o_ref[...] = _ffn_body(x_ref[...], g_ref[...], wi_ref, wo_ref)


def _ffn(x, g, w_in, w_out, tm_pref=512):
    m, d = x.shape
    tm = _row_tile(m, tm_pref)
    resident = lambda a: pl.BlockSpec(a.shape, lambda i: (0, 0), pipeline_mode=pl.Buffered(1))
    return pl.pallas_call(
        _ffn_kernel,
        grid=(m // tm,),
        in_specs=[pl.BlockSpec((tm, d), lambda i: (i, 0)), pl.BlockSpec((1, d), lambda i: (0, 0)),
                  resident(w_in), resident(w_out)],
        out_specs=pl.BlockSpec((tm, d), lambda i: (i, 0)),
        out_shape=jax.ShapeDtypeStruct((m, d), F32),
        compiler_params=_cparams(("parallel",), VMEM_LIMIT),
        name="ffn",
    )(x, g.reshape(1, d), w_in, w_out)


def _proj_gdn_kernel(x_ref, g_ref, wqkv_ref, wz_ref, wba_ref, qkv_ref, z_ref, ba_ref):
    h = _rms(x_ref[...], g_ref[...]).astype(BF16)
    qkv_ref[...] = jnp.dot(h, wqkv_ref[...], preferred_element_type=F32)
    z_ref[...] = jnp.dot(h, wz_ref[...], preferred_element_type=F32)
    ba_ref[...] = jnp.dot(h, wba_ref[...], preferred_element_type=F32)


def _proj_gdn(x, g, wqkv, wz, wba, tm_pref=512):
    m, d = x.shape
    tm = _row_tile(m, tm_pref)
    row = lambda w: pl.BlockSpec((tm, w), lambda i: (i, 0))
    full = lambda a: pl.BlockSpec(a.shape, lambda i: (0, 0))
    return pl.pallas_call(
        _proj_gdn_kernel,
        grid=(m // tm,),
        in_specs=[row(d), full(g), full(wqkv), full(wz), full(wba)],
        out_specs=[row(wqkv.shape[1]), row(wz.shape[1]), row(wba.shape[1])],
        out_shape=[jax.ShapeDtypeStruct((m, w.shape[1]), F32) for w in (wqkv, wz, wba)],
        compiler_params=_cparams(("parallel",), VMEM_LIMIT),
        name="proj_gdn",
    )(x, g, wqkv, wz, wba)


def _rope_table_kernel(inv_ref, invc_ref, cos_ref, sin_ref, cost_ref, sint_ref, *, pos0, tm):
    i = pl.program_id(0)
    half = HALF_W // 2
    pos = lax.broadcasted_iota(jnp.int32, (tm, HEAD_W), 0) + (i * tm + pos0)
    ang = pos.astype(F32) * inv_ref[...]
    first = (lax.broadcasted_iota(jnp.int32, (tm, HEAD_W), 1) % HALF_W) < half
    s = jnp.sin(ang)
    cos_ref[...] = jnp.cos(ang)
    sin_ref[...] = jnp.where(first, -s, s)
    pos_t = lax.broadcasted_iota(jnp.int32, (HEAD_W, tm), 1) + (i * tm + pos0)
    ang_t = pos_t.astype(F32) * invc_ref[...]
    first_t = (lax.broadcasted_iota(jnp.int32, (HEAD_W, tm), 0) % HALF_W) < half
    s_t = jnp.sin(ang_t)
    cost_ref[...] = jnp.cos(ang_t)
    sint_ref[...] = jnp.where(first_t, -s_t, s_t)


def _rope_table(pos0, n):
    half = HALF_W // 2
    inv = ROPE_THETA ** (-jnp.arange(half, dtype=F32) / half)
    inv_rep = jnp.tile(inv, HEAD_W // half)
    tm = _row_tile(n, 512)
    return pl.pallas_call(
        functools.partial(_rope_table_kernel, pos0=pos0, tm=tm),
        grid=(n // tm,),
        in_specs=[pl.BlockSpec((1, HEAD_W), lambda i: (0, 0)), pl.BlockSpec((HEAD_W, 1), lambda i: (0, 0))],
        out_specs=[pl.BlockSpec((tm, HEAD_W), lambda i: (i, 0))] * 2
                  + [pl.BlockSpec((HEAD_W, tm), lambda i: (0, i))] * 2,
        out_shape=[jax.ShapeDtypeStruct((n, HEAD_W), F32)] * 2 + [jax.ShapeDtypeStruct((HEAD_W, n), F32)] * 2,
        compiler_params=_cparams(("parallel",)),
        name="rope_table",
    )(inv_rep.reshape(1, HEAD_W), inv_rep.reshape(HEAD_W, 1))


def _rope_rows(p, cos, sin, scale, out_refs):
    half = HALF_W // 2
    first = (lax.broadcasted_iota(jnp.int32, cos.shape, 1) % HALF_W) < half
    for hd in range(N_HEAD):
        xh = p[:, hd * HEAD_W:(hd + 1) * HEAD_W]
        partner = jnp.where(first, pltpu.roll(xh, HEAD_W - half, 1), pltpu.roll(xh, half, 1))
        y = (xh * cos + partner * sin) * scale
        for o in out_refs:
            if len(o.shape) == 3:
                o[:, hd, :] = y.astype(o.dtype)
            else:
                o[:, hd * HEAD_W:(hd + 1) * HEAD_W] = y.astype(o.dtype)


def _proj_attn_dec_kernel(x_ref, g_ref, wq_ref, wk_ref, wv_ref, cos_ref, sin_ref, q_ref, k_ref, v_ref):
    h = _rms(x_ref[...], g_ref[...]).astype(BF16)
    cos = cos_ref[...]
    sin = sin_ref[...]
    _rope_rows(jnp.dot(h, wq_ref[...], preferred_element_type=F32), cos, sin, HALF_W ** -0.5, (q_ref,))
    _rope_rows(jnp.dot(h, wk_ref[...], preferred_element_type=F32), cos, sin, 1.0, (k_ref,))
    v_ref[...] = jnp.dot(h, wv_ref[...], preferred_element_type=F32)


def _proj_attn_dec(x, g, wq, wk, wv, cos, sin):
    vm = pl.BlockSpec(memory_space=pltpu.VMEM)
    m = x.shape[0]
    return pl.pallas_call(
        _proj_attn_dec_kernel,
        in_specs=[vm] * 7,
        out_specs=[vm] * 3,
        out_shape=[jax.ShapeDtypeStruct((m, BRANCH_W), F32)] * 3,
        compiler_params=pltpu.CompilerParams(vmem_limit_bytes=VMEM_LIMIT),
        name="proj_attn_decode",
    )(x, g, wq, wk, wv, cos, sin)


Q_SCALE_LOG2 = (HALF_W ** -0.5) * math.log2(math.e)


def _proj_attn_kernel(x_ref, g_ref, wqt_ref, wk_ref, wv_ref, wvt_ref, cos_ref, sin_ref, cost_ref, sint_ref,
                      *rest):
    qt_ref, k_ref, k16_ref, v_ref, vt_ref = rest[-5:]
    h = _rms(x_ref[...], g_ref[...]).astype(BF16)
    nt = (((1,), (1,)), ((), ()))
    _rope_rows(jnp.dot(h, wk_ref[...], preferred_element_type=F32), cos_ref[...], sin_ref[...], 1.0,
               (k_ref, k16_ref))
    v = jnp.dot(h, wv_ref[...], preferred_element_type=F32)
    for hd in range(N_HEAD):
        v_ref[:, hd, :] = v[:, hd * HEAD_W:(hd + 1) * HEAD_W]
    vt_ref[...] = lax.dot_general(wvt_ref[...], h, nt, preferred_element_type=F32).astype(vt_ref.dtype)
    qt = lax.dot_general(wqt_ref[...], h, nt, preferred_element_type=F32)
    cos_t = cost_ref[...]
    sin_t = sint_ref[...]
    half = HALF_W // 2
    for hd in range(N_HEAD):
        xh = qt[hd * HEAD_W:(hd + 1) * HEAD_W]
        partner = jnp.concatenate([xh[half:2 * half], xh[0:half], xh[3 * half:4 * half], xh[2 * half:3 * half]],
                                  axis=0)
        qt_ref[hd * HEAD_W:(hd + 1) * HEAD_W, :] = ((xh * cos_t + partner * sin_t) * Q_SCALE_LOG2).astype(qt_ref.dtype)


def _proj_attn(x, g, wqt, wk, wv, wvt, cos, sin, cos_t, sin_t, rows_per_seq, layer, depth, kv_bufs, tm_pref=512):
    m, d = x.shape
    tm = _row_tile(rows_per_seq, tm_pref)
    n_t = rows_per_seq // tm
    row = lambda w: pl.BlockSpec((tm, w), lambda i: (i, 0))
    row3 = pl.BlockSpec((None, tm, N_HEAD, HEAD_W), lambda i: (layer, i, 0, 0))
    colb = pl.BlockSpec((BRANCH_W, tm), lambda i: (0, i))
    full = lambda a: pl.BlockSpec(a.shape, lambda i: (0, 0))
    tab = pl.BlockSpec((tm, HEAD_W), lambda i: (i % n_t, 0))
    tab_t = pl.BlockSpec((HEAD_W, tm), lambda i: (0, i % n_t))
    in_specs = [row(d), full(g), full(wqt), full(wk), full(wv), full(wvt), tab, tab, tab_t, tab_t]
    args = [x, g, wqt, wk, wv, wvt, cos, sin, cos_t, sin_t]
    aliases = {}
    if kv_bufs is not None:
        aliases = {len(args): 1, len(args) + 1: 3}
        in_specs += [pl.BlockSpec(memory_space=pl.ANY)] * 2
        args += list(kv_bufs)
    buf = jax.ShapeDtypeStruct((depth, m, N_HEAD, HEAD_W), F32)
    return pl.pallas_call(
        _proj_attn_kernel,
        grid=(m // tm,),
        in_specs=in_specs,
        out_specs=[colb, row3, row(BRANCH_W), row3, colb],
        out_shape=[jax.ShapeDtypeStruct((BRANCH_W, m), BF16), buf,
                   jax.ShapeDtypeStruct((m, BRANCH_W), BF16), buf,
                   jax.ShapeDtypeStruct((BRANCH_W, m), BF16)],
        input_output_aliases=aliases,
        compiler_params=_cparams(("parallel",), VMEM_LIMIT),
        name="proj_attn",
    )(*args)


def _proj_cg_kernel(x_ref, g_ref, wa_ref, wb_ref, wgate_ref, u_ref, gates_ref):
    h = _rms(x_ref[...], g_ref[...]).astype(BF16)
    a = jnp.dot(h, wa_ref[...], preferred_element_type=F32)
    b = jnp.dot(h, wb_ref[...], preferred_element_type=F32)
    u_ref[...] = a * _sigmoid(b)
    gates_ref[...] = _sigmoid(jnp.dot(h, wgate_ref[...], preferred_element_type=F32)).astype(gates_ref.dtype)


def _proj_cg(x, g, wa, wb, wgate, tm_pref=512):
    m, d = x.shape
    tm = _row_tile(m, tm_pref)
    row = lambda w: pl.BlockSpec((tm, w), lambda i: (i, 0))
    full = lambda a: pl.BlockSpec(a.shape, lambda i: (0, 0))
    return pl.pallas_call(
        _proj_cg_kernel,
        grid=(m // tm,),
        in_specs=[row(d), full(g), full(wa), full(wb), full(wgate)],
        out_specs=[row(wa.shape[1]), row(wgate.shape[1])],
        out_shape=[jax.ShapeDtypeStruct((m, wa.shape[1]), F32),
                   jax.ShapeDtypeStruct((m, wgate.shape[1]), BF16)],
        compiler_params=_cparams(("parallel",), VMEM_LIMIT),
        name="proj_cg",
    )(x, g, wa, wb, wgate)


def _l2norm(x):
    return x * lax.rsqrt(jnp.sum(x * x, axis=-1, keepdims=True) + EPS)


def _split2(x):
    hi = x.astype(BF16)
    return hi, (x - hi.astype(F32)).astype(BF16)


def _split3(x):
    hi = x.astype(BF16)
    r1 = x - hi.astype(F32)
    mid = r1.astype(BF16)
    return hi, mid, (r1 - mid.astype(F32)).astype(BF16)


def _dot3(a, b):
    a_hi, a_lo = _split2(a)
    b_hi, b_lo = _split2(b)
    return (jnp.dot(a_hi, b_hi, preferred_element_type=F32) + jnp.dot(a_hi, b_lo, preferred_element_type=F32)
            + jnp.dot(a_lo, b_hi, preferred_element_type=F32))


def _gdn_gates(ba, arow, dtrow):
    beta = _sigmoid(ba)
    g = -jnp.exp(arow) * _softplus(ba + dtrow)
    return beta, g


def _gdn_kernel(qkv_ref, z_ref, ba_ref, cw_ref, arow_ref, dtrow_ref, gn_ref,
                o_ref, s_out_ref, ext_ref, s_ref, *, tm):
    t = pl.program_id(1)
    nc = tm // CHUNK
    qkv_w = N_HEAD * HEAD_W

    @pl.when(t == 0)
    def _():
        ext_ref[0:8, :] = jnp.zeros((8, ext_ref.shape[1]), F32)
        s_ref[...] = jnp.zeros(s_ref.shape, F32)

    ext_ref[8:8 + tm, :] = qkv_ref[...]
    y = cw_ref[CONV_A - 1:CONV_A, :] * ext_ref[8:8 + tm, :]
    for j in range(CONV_A - 1):
        off = 8 - (CONV_A - 1) + j
        y = y + cw_ref[j:j + 1, :] * ext_ref[off:off + tm, :]
    ext_ref[0:8, :] = ext_ref[tm:tm + 8, :]
    qkv = _silu(y)

    beta, g_all = _gdn_gates(ba_ref[...], arow_ref[...], dtrow_ref[...])
    r = lax.broadcasted_iota(jnp.int32, (tm, tm), 0)
    c = lax.broadcasted_iota(jnp.int32, (tm, tm), 1)
    same = r // CHUNK == c // CHUNK
    lbd = jnp.where(same, jnp.where(c <= r, 1.0, 0.0), 0.0).astype(BF16)
    gc = None
    for piece in _split3(g_all):
        term = jnp.dot(lbd, piece, preferred_element_type=F32)
        gc = term if gc is None else gc + term
    gc_t = gc.T

    ri = lax.broadcasted_iota(jnp.int32, (CHUNK, tm), 0)
    rc = lax.broadcasted_iota(jnp.int32, (CHUNK, tm), 1)
    eye_row = jnp.where(ri == rc % CHUNK, 1.0, 0.0)

    def to_bd(x_row):
        return jnp.where(same, jnp.concatenate([x_row] * nc, axis=0), 0.0)

    def from_bd(x_bd):
        acc = x_bd[0:CHUNK]
        for ch in range(1, nc):
            acc = acc + x_bd[ch * CHUNK:(ch + 1) * CHUNK]
        return acc

    gn = gn_ref[...]
    nt = (((1,), (1,)), ((), ()))
    heads = range(N_HEAD)
    qh = [_l2norm(qkv[:, hd * HEAD_W:(hd + 1) * HEAD_W]) * (HEAD_W ** -0.5) for hd in heads]
    kh = [_l2norm(qkv[:, qkv_w + hd * HEAD_W:qkv_w + (hd + 1) * HEAD_W]) for hd in heads]
    vh = [qkv[:, 2 * qkv_w + hd * HEAD_W:2 * qkv_w + (hd + 1) * HEAD_W] for hd in heads]
    gcol = [gc[:, N_HEAD + hd:N_HEAD + hd + 1] for hd in heads]
    grow = [gc_t[N_HEAD + hd:N_HEAD + hd + 1, :] for hd in heads]
    bcol = [beta[:, hd:hd + 1] for hd in heads]
    egc = [jnp.exp(g) for g in gcol]
    k16 = [k.astype(BF16) for k in kh]
    lower = jnp.logical_and(same, r >= c)
    decay = [jnp.exp(jnp.where(lower, gcol[hd] - grow[hd], NEG_INF)) for hd in heads]
    kk = [lax.dot_general(k16[hd], k16[hd], nt, preferred_element_type=F32) for hd in heads]
    qk = [lax.dot_general(qh[hd].astype(BF16), k16[hd], nt, preferred_element_type=F32) for hd in heads]
    n_bd = [jnp.where(r > c, -(kk[hd] * decay[hd] * bcol[hd]), 0.0) for hd in heads]
    attn16 = [(qk[hd] * decay[hd]).astype(BF16) for hd in heads]
    n_row = [from_bd(n) for n in n_bd]
    s_inv = [eye_row + n for n in n_row]
    x_row = [_dot3(n_row[hd], n_bd[hd]) for hd in heads]
    for _ in range(4):
        res = [_dot3(jnp.concatenate([s_inv[hd], x_row[hd]], axis=0), to_bd(x_row[hd])) for hd in heads]
        s_inv = [s_inv[hd] + res[hd][0:CHUNK] for hd in heads]
        x_row = [res[hd][CHUNK:2 * CHUNK] for hd in heads]
    s_inv = [s_inv[hd] + _dot3(s_inv[hd], to_bd(x_row[hd])) for hd in heads]
    sol = [_dot3(to_bd(s_inv[hd]), jnp.concatenate([vh[hd] * bcol[hd], kh[hd] * (bcol[hd] * egc[hd])], axis=1))
           for hd in heads]
    qg16 = [(qh[hd] * egc[hd]).astype(BF16) for hd in heads]
    w16 = [sol[hd][:, HEAD_W:].astype(BF16) for hd in heads]
    s = [s_ref[hd] for hd in heads]
    v_news = [[] for _ in heads]
    o_inters = [[] for _ in heads]
    for ch in range(nc):
        sl = slice(ch * CHUNK, (ch + 1) * CHUNK)
        for hd in heads:
            g_last = gcol[hd][ch * CHUNK + CHUNK - 1:(ch + 1) * CHUNK]
            res = jnp.dot(jnp.concatenate([w16[hd][sl], qg16[hd][sl]], axis=0), s[hd].astype(BF16),
                          preferred_element_type=F32)
            v_new = sol[hd][sl, :HEAD_W] - res[0:CHUNK]
            o_inters[hd].append(res[CHUNK:2 * CHUNK])
            v_news[hd].append(v_new)
            kd16 = (kh[hd][sl] * jnp.exp(g_last - gcol[hd][sl])).astype(BF16)
            s[hd] = s[hd] * jnp.exp(g_last) + lax.dot_general(kd16, v_new.astype(BF16), (((0,), (0,)), ((), ())),
                                                              preferred_element_type=F32)
    for hd in heads:
        s_ref[hd] = s[hd]
        o = jnp.concatenate(o_inters[hd], axis=0) + jnp.dot(
            attn16[hd], jnp.concatenate(v_news[hd], axis=0).astype(BF16), preferred_element_type=F32)
        zc = z_ref[:, hd * HEAD_W:(hd + 1) * HEAD_W]
        o_ref[:, hd * HEAD_W:(hd + 1) * HEAD_W] = (_rms(o, gn) * _silu(zc)).astype(o_ref.dtype)

    @pl.when(t == pl.num_programs(1) - 1)
    def _():
        s_out_ref[0] = s_ref[...]


def _gdn(qkv, z, ba, conv_w, arow, dtrow, gnorm, n_seq, tm_pref=256):
    m, w = qkv.shape
    t_len = m // n_seq
    tm = _row_tile(t_len, tm_pref)
    n_t = t_len // tm
    row = lambda wd: pl.BlockSpec((tm, wd), lambda b, t: (b * n_t + t, 0))
    full = lambda a: pl.BlockSpec(a.shape, lambda b, t: (0, 0))
    return pl.pallas_call(
        functools.partial(_gdn_kernel, tm=tm),
        grid=(n_seq, n_t),
        in_specs=[row(w), row(BRANCH_W), row(HEAD_W), full(conv_w), full(arow), full(dtrow), full(gnorm)],
        out_specs=[row(BRANCH_W),
                   pl.BlockSpec((1, N_HEAD, HEAD_W, HEAD_W), lambda b, t: (b, 0, 0, 0))],
        out_shape=[jax.ShapeDtypeStruct((m, BRANCH_W), BF16),
                   jax.ShapeDtypeStruct((n_seq, N_HEAD, HEAD_W, HEAD_W), F32)],
        scratch_shapes=[pltpu.VMEM((tm + 8, w), F32), pltpu.VMEM((N_HEAD, HEAD_W, HEAD_W), F32)],
        compiler_params=_cparams(("parallel", "arbitrary"), VMEM_LIMIT),
        name="gdn",
    )(qkv, z, ba, conv_w, arow, dtrow, gnorm)


def _lambda(lv, lam_init):
    s01 = jnp.sum(lv[0:1] * lv[1:2], axis=-1, keepdims=True)
    s23 = jnp.sum(lv[2:3] * lv[3:4], axis=-1, keepdims=True)
    return jnp.exp(s01) - jnp.exp(s23) + lam_init


KV_GROUP = 4


def _attn_kernel(qt_ref, k_ref, vt_ref, lv_ref, dn_ref, o_ref,
                 q1_ref, q2_ref, m1, l1, a1, m2, l2, a2, *, tq, lam_init):
    qi = pl.program_id(2)
    qt = qt_ref[...]
    chan = lax.broadcasted_iota(jnp.int32, qt.shape, 0)
    q1_ref[...] = jnp.where(chan < HALF_W, qt, jnp.zeros_like(qt))
    q2_ref[...] = jnp.where(chan >= HALF_W, qt, jnp.zeros_like(qt))
    maps = ((q1_ref, m1, l1, a1), (q2_ref, m2, l2, a2))
    for _, m, l, a in maps:
        m[...] = jnp.full(m.shape, NEG_INF, F32)
        l[...] = jnp.zeros(l.shape, F32)
        a[...] = jnp.zeros(a.shape, F32)

    def tiles(ki, n_tile, masked):
        offs = [pl.multiple_of((ki + j) * tq, tq) for j in range(n_tile)]
        k16 = [k_ref[pl.ds(off, tq), :] for off in offs]
        vt16 = [vt_ref[:, pl.ds(off, tq)] for off in offs]
        sts = [[jnp.dot(k, qr[...], preferred_element_type=F32) for k in k16] for qr, _, _, _ in maps]
        olds = [(m[...], l[...], a[...]) for _, m, l, a in maps]
        for st_list, (m_old, l_old, a_old), (_, m, l, a) in zip(sts, olds, maps):
            if masked:
                kv = lax.broadcasted_iota(jnp.int32, st_list[0].shape, 0)
                qq = lax.broadcasted_iota(jnp.int32, st_list[0].shape, 1)
                st_list = [jnp.where(kv <= qq, st, NEG_INF) for st in st_list]
            m_new = m_old
            for st in st_list:
                m_new = jnp.maximum(m_new, jnp.max(st, axis=0, keepdims=True))
            alpha = jnp.exp2(m_old - m_new)
            l_new = alpha * l_old
            a_new = alpha * a_old
            for st, vt in zip(st_list, vt16):
                p = jnp.exp2(st - m_new)
                l_new = l_new + jnp.sum(p, axis=0, keepdims=True)
                a_new = a_new + jnp.dot(vt, p.astype(BF16), preferred_element_type=F32)
            l[...] = l_new
            a[...] = a_new
            m[...] = m_new

    def body(kp, carry):
        tiles(KV_GROUP * kp, KV_GROUP, False)
        return carry

    n_group = qi // KV_GROUP
    lax.fori_loop(0, n_group, body, 0)
    done = n_group * KV_GROUP
    size = KV_GROUP // 2
    while size >= 1:
        take = ((qi - done) // size) > 0

        @pl.when(take)
        def _(done=done, size=size):
            tiles(done, size, False)

        done = done + jnp.where(take, size, 0)
        size //= 2
    tiles(qi, 1, True)

    lam = _lambda(lv_ref[...], lam_init)
    o = a1[...] / l1[...] - lam * (a2[...] / l2[...])
    o = o * lax.rsqrt(jnp.mean(o * o, axis=0, keepdims=True) + EPS) * dn_ref[...] * (1.0 - lam_init)
    o_ref[...] = o.T.astype(o_ref.dtype)


def _attn(qt, k16, vt, lv, dnorm_col, n_seq, lam_init, tq_pref=512):
    m = k16.shape[0]
    t_len = m // n_seq
    tq = _row_tile(t_len, tq_pref)
    n_q = t_len // tq
    full = lambda a: pl.BlockSpec(a.shape, lambda b, h, i: (0, 0))
    stat = pltpu.VMEM((1, tq), F32)
    accs = pltpu.VMEM((HEAD_W, tq), F32)
    return pl.pallas_call(
        functools.partial(_attn_kernel, tq=tq, lam_init=lam_init),
        grid=(n_seq, N_HEAD, n_q),
        in_specs=[pl.BlockSpec((HEAD_W, tq), lambda b, h, i: (h, b * n_q + i)),
                  pl.BlockSpec((t_len, HEAD_W), lambda b, h, i: (b, h)),
                  pl.BlockSpec((HEAD_W, t_len), lambda b, h, i: (h, b)),
                  full(lv), full(dnorm_col)],
        out_specs=pl.BlockSpec((tq, HEAD_W), lambda b, h, i: (b * n_q + i, h)),
        out_shape=jax.ShapeDtypeStruct((m, BRANCH_W), BF16),
        scratch_shapes=[pltpu.VMEM((HEAD_W, tq), BF16), pltpu.VMEM((HEAD_W, tq), BF16),
                        stat, stat, accs, stat, stat, accs],
        compiler_params=_cparams(("parallel", "parallel", "arbitrary"), VMEM_LIMIT),
        name="diff_attn",
    )(qt, k16, vt, lv, dnorm_col)


def _ln_swish(y, g, b):
    yc = y - jnp.mean(y, axis=-1, keepdims=True)
    yn = yc * lax.rsqrt(jnp.mean(yc * yc, axis=-1, keepdims=True) + EPS) * g + b
    return _silu(yn)


CARRY_C = 32


def _conf_kernel(u_ref, w_ref, b_ref, lg_ref, lb_ref, o_ref, ext_ref, shift_ref, *, tm):
    t = pl.program_id(1)

    @pl.when(t == 0)
    def _():
        ext_ref[0:CARRY_C, :] = jnp.zeros((CARRY_C, ext_ref.shape[1]), F32)

    ext_ref[CARRY_C:CARRY_C + tm, :] = u_ref[...]
    base = CARRY_C - (CONV_C - 1)
    acc = None
    for phase in range(8):
        starts = [base + j for j in range(CONV_C) if (base + j) % 8 == phase]
        span = max(starts) - phase + tm
        shift_ref[0:span, :] = ext_ref[phase:phase + span, :]
        for st in starts:
            term = w_ref[st - base:st - base + 1, :] * shift_ref[st - phase:st - phase + tm, :]
            acc = term if acc is None else acc + term
    ext_ref[0:CARRY_C, :] = ext_ref[tm:tm + CARRY_C, :]
    o_ref[...] = _ln_swish(acc + b_ref[...], lg_ref[...], lb_ref[...]).astype(o_ref.dtype)


def _conf(u, w, b, lg, lb, n_seq, tm_pref=256):
    m, d = u.shape
    t_len = m // n_seq
    tm = _row_tile(t_len, tm_pref)
    n_t = t_len // tm
    row = pl.BlockSpec((tm, d), lambda s, t: (s * n_t + t, 0))
    full = lambda a: pl.BlockSpec(a.shape, lambda s, t: (0, 0))
    return pl.pallas_call(
        functools.partial(_conf_kernel, tm=tm),
        grid=(n_seq, n_t),
        in_specs=[row, full(w), full(b), full(lg), full(lb)],
        out_specs=row,
        out_shape=jax.ShapeDtypeStruct((m, d), BF16),
        scratch_shapes=[pltpu.VMEM((tm + CARRY_C, d), F32), pltpu.VMEM((tm + CARRY_C, d), F32)],
        compiler_params=_cparams(("parallel", "arbitrary")),
        name="conf_conv",
    )(u, w, b, lg, lb)


def _merge_ffn_kernel(x_ref, oa_ref, ob_ref, oc_ref, gates_ref, wbr_ref, wom_ref, g_ref, wi_ref, wo_ref, gf_ref,
                      out_ref, *, final_norm):
    mix = None
    for b, o in enumerate((oa_ref, ob_ref, oc_ref)):
        pr = jnp.dot(o[...], wbr_ref[b], preferred_element_type=F32)
        term = gates_ref[:, b * D_MODEL:(b + 1) * D_MODEL] * pr
        mix = term if mix is None else mix + term
    x = x_ref[...] + jnp.dot(mix.astype(BF16), wom_ref[...], preferred_element_type=F32)
    y = _ffn_body(x, g_ref[...], wi_ref, wo_ref)
    out_ref[...] = _rms(y, gf_ref[...]) if final_norm else y


def _merge_ffn(x, oa, ob, oc, gates, wbr, wom, g, w_in, w_out, g_final, final_norm, tm_pref=512):
    m, d = x.shape
    tm = _row_tile(m, tm_pref)
    row = lambda w: pl.BlockSpec((tm, w), lambda i: (i, 0))
    vec = pl.BlockSpec((1, d), lambda i: (0, 0))
    resident = lambda a: pl.BlockSpec(a.shape, lambda i: (0,) * a.ndim, pipeline_mode=pl.Buffered(1))
    return pl.pallas_call(
        functools.partial(_merge_ffn_kernel, final_norm=final_norm),
        grid=(m // tm,),
        in_specs=[row(d), row(BRANCH_W), row(BRANCH_W), row(BRANCH_W), row(3 * d),
                  resident(wbr), resident(wom), vec, resident(w_in), resident(w_out), vec],
        out_specs=row(d),
        out_shape=jax.ShapeDtypeStruct((m, d), F32),
        compiler_params=_cparams(("parallel",), VMEM_LIMIT),
        name="merge_ffn",
    )(x, oa, ob, oc, gates, wbr, wom, g.reshape(1, d), w_in, w_out, g_final.reshape(1, d))


def _gdn_dec_kernel(qkv_ref, z_ref, ba_ref, st_ref, s0_ref, cw_ref, arow_ref, dtrow_ref, gn_ref,
                    o_ref, st_out_ref, s_out_ref, *, bt):
    qkv_w = N_HEAD * HEAD_W
    x = qkv_ref[...]
    y = cw_ref[CONV_A - 1:CONV_A, :] * x
    for j in range(CONV_A - 1):
        y = y + cw_ref[j:j + 1, :] * st_ref[j]
    for j in range(CONV_A - 2):
        st_out_ref[j] = st_ref[j + 1]
    st_out_ref[CONV_A - 2] = x
    qkv = _silu(y)
    beta, g_all = _gdn_gates(ba_ref[...], arow_ref[...], dtrow_ref[...])
    gn = gn_ref[...]
    pad = jnp.zeros((HEAD_W - bt, HEAD_W), F32)
    for hd in range(N_HEAD):
        qh = _l2norm(qkv[:, hd * HEAD_W:(hd + 1) * HEAD_W]) * (HEAD_W ** -0.5)
        kh = _l2norm(qkv[:, qkv_w + hd * HEAD_W:qkv_w + (hd + 1) * HEAD_W])
        vh = qkv[:, 2 * qkv_w + hd * HEAD_W:2 * qkv_w + (hd + 1) * HEAD_W]
        bcol = beta[:, hd:hd + 1]
        eg = jnp.exp(g_all[:, N_HEAD + hd:N_HEAD + hd + 1])
        qk = jnp.sum(qh * kh, axis=-1, keepdims=True)
        w_t = jnp.concatenate([kh * (bcol * eg), pad], axis=0).T
        q_t = jnp.concatenate([qh * eg, pad], axis=0).T
        k_t = jnp.concatenate([kh, pad], axis=0).T
        for b in range(bt):
            s = s0_ref[b, hd]
            ws = jnp.sum(w_t[:, b:b + 1] * s, axis=0, keepdims=True)
            qs = jnp.sum(q_t[:, b:b + 1] * s, axis=0, keepdims=True)
            v_new = vh[b:b + 1] * bcol[b:b + 1] - ws
            o = qs + qk[b:b + 1] * v_new
            s_out_ref[b, hd] = s * eg[b:b + 1] + k_t[:, b:b + 1] * v_new
            zc = z_ref[b:b + 1, hd * HEAD_W:(hd + 1) * HEAD_W]
            o_ref[b:b + 1, hd * HEAD_W:(hd + 1) * HEAD_W] = _rms(o, gn) * _silu(zc)


def _gdn_dec(qkv, z, ba, st, s0, conv_w, arow, dtrow, gnorm, bt=8):
    n, w = qkv.shape
    row = lambda wd: pl.BlockSpec((bt, wd), lambda i: (i, 0))
    full = lambda a: pl.BlockSpec(a.shape, lambda i: (0, 0))
    st_spec = pl.BlockSpec((CONV_A - 1, bt, w), lambda i: (0, i, 0))
    s_spec = pl.BlockSpec((bt, N_HEAD, HEAD_W, HEAD_W), lambda i: (i, 0, 0, 0))
    return pl.pallas_call(
        functools.partial(_gdn_dec_kernel, bt=bt),
        grid=(n // bt,),
        in_specs=[row(w), row(BRANCH_W), row(HEAD_W), st_spec, s_spec,
                  full(conv_w), full(arow), full(dtrow), full(gnorm)],
        out_specs=[row(BRANCH_W), st_spec, s_spec],
        out_shape=[jax.ShapeDtypeStruct((n, BRANCH_W), F32),
                   jax.ShapeDtypeStruct(st.shape, F32),
                   jax.ShapeDtypeStruct(s0.shape, F32)],
        compiler_params=_cparams(("parallel",)),
        name="gdn_decode",
    )(qkv, z, ba, st, s0, conv_w, arow, dtrow, gnorm)


def _attn_dec_kernel(pt_ref, q_ref, kn_ref, vn_ref, lv_ref, dn_ref, *rest, n_page, lam_init):
    k_refs = rest[:n_page]
    v_refs = rest[n_page:2 * n_page]
    o_ref, m_ref, l_ref, a_ref = rest[2 * n_page:]
    j = pl.program_id(1)

    @pl.when(j == 0)
    def _():
        m_ref[...] = jnp.full(m_ref.shape, NEG_INF, F32)
        l_ref[...] = jnp.zeros(l_ref.shape, F32)
        a_ref[...] = jnp.zeros(a_ref.shape, F32)

    q4 = q_ref[0]
    lane = lax.broadcasted_iota(jnp.int32, q4.shape, 1)
    qmat = jnp.concatenate([jnp.where(lane < HALF_W, q4, 0.0), jnp.where(lane >= HALF_W, q4, 0.0)], axis=0)
    qmat16 = qmat.astype(BF16)
    rows = PAGE * N_HEAD
    r = lax.broadcasted_iota(jnp.int32, (2 * N_HEAD, n_page * rows), 0)
    c = lax.broadcasted_iota(jnp.int32, (2 * N_HEAD, n_page * rows), 1)
    valid = (c % N_HEAD) == (r % N_HEAD)
    sc = jnp.concatenate(
        [lax.dot_general(qmat16, k_refs[p][...].astype(BF16), (((1,), (1,)), ((), ())),
                         preferred_element_type=F32) for p in range(n_page)], axis=1)
    sc = jnp.where(valid, sc, NEG_INF)
    m_old = m_ref[...]
    m_new = jnp.maximum(m_old, jnp.max(sc, axis=-1, keepdims=True))
    alpha = jnp.exp(m_old - m_new)
    pr = jnp.exp(sc - m_new)
    l_ref[...] = alpha * l_ref[...] + jnp.sum(pr, axis=-1, keepdims=True)
    pr = pr.astype(BF16)
    pv = None
    for p in range(n_page):
        term = jnp.dot(pr[:, p * rows:(p + 1) * rows], v_refs[p][...].astype(BF16), preferred_element_type=F32)
        pv = term if pv is None else pv + term
    a_ref[...] = alpha * a_ref[...] + pv
    m_ref[...] = m_new

    @pl.when(j == pl.num_programs(1) - 1)
    def _():
        k8 = jnp.concatenate([kn_ref[0], kn_ref[0]], axis=0)
        v8 = jnp.concatenate([vn_ref[0], vn_ref[0]], axis=0)
        s_self = jnp.sum(qmat * k8, axis=-1, keepdims=True)
        m_new = jnp.maximum(m_ref[...], s_self)
        alpha = jnp.exp(m_ref[...] - m_new)
        p_self = jnp.exp(s_self - m_new)
        l_fin = alpha * l_ref[...] + p_self
        a_fin = alpha * a_ref[...] + p_self * v8
        on = a_fin / l_fin
        lam = _lambda(lv_ref[...], lam_init)
        o = on[0:N_HEAD] - lam * on[N_HEAD:2 * N_HEAD]
        o_ref[0] = _rms(o, dn_ref[...]) * (1.0 - lam_init)


def _attn_dec(q, k_new, v_new, cache_k, cache_v, page_flat, layer, lv, dnorm, lam_init, n_page=16):
    n = q.shape[0]
    pages_per_seq = page_flat.shape[0] // n
    n_page = n_page if pages_per_seq % n_page == 0 else 1
    tok = pl.BlockSpec((1, N_HEAD, HEAD_W), lambda b, j, pt: (b, 0, 0))
    full = lambda a: pl.BlockSpec(a.shape, lambda b, j, pt: (0, 0))

    def page_spec(p):
        return pl.BlockSpec((None, None, PAGE * N_HEAD, HEAD_W),
                            lambda b, j, pt: (layer, pt[b * pages_per_seq + j * n_page + p], 0, 0))

    return pl.pallas_call(
        functools.partial(_attn_dec_kernel, n_page=n_page, lam_init=lam_init),
        grid_spec=pltpu.PrefetchScalarGridSpec(
            num_scalar_prefetch=1,
            grid=(n, pages_per_seq // n_page),
            in_specs=[tok, tok, tok, full(lv), full(dnorm)]
                     + [page_spec(p) for p in range(n_page)] * 2,
            out_specs=tok,
            scratch_shapes=[pltpu.VMEM((2 * N_HEAD, 1), F32), pltpu.VMEM((2 * N_HEAD, 1), F32),
                            pltpu.VMEM((2 * N_HEAD, HEAD_W), F32)],
        ),
        out_shape=jax.ShapeDtypeStruct((n, N_HEAD, HEAD_W), F32),
        compiler_params=_cparams(("parallel", "arbitrary")),
        name="diff_attn_decode",
    )(page_flat, q.reshape(n, N_HEAD, HEAD_W), k_new.reshape(n, N_HEAD, HEAD_W),
      v_new.reshape(n, N_HEAD, HEAD_W), lv, dnorm,
      *([cache_k] * n_page), *([cache_v] * n_page))


def _conf_dec_kernel(u_ref, st_ref, w_ref, b_ref, lg_ref, lb_ref, o_ref, st_out_ref):
    u = u_ref[...]
    acc = w_ref[CONV_C - 1:CONV_C, :] * u
    for j in range(CONV_C - 1):
        acc = acc + w_ref[j:j + 1, :] * st_ref[j]
    for j in range(CONV_C - 2):
        st_out_ref[j] = st_ref[j + 1]
    st_out_ref[CONV_C - 2] = u
    o_ref[...] = _ln_swish(acc + b_ref[...], lg_ref[...], lb_ref[...])


def _conf_dec(u, st, w, b, lg, lb):
    vm = pl.BlockSpec(memory_space=pltpu.VMEM)
    return pl.pallas_call(
        _conf_dec_kernel,
        in_specs=[vm] * 6,
        out_specs=[vm, vm],
        out_shape=[jax.ShapeDtypeStruct(u.shape, F32), jax.ShapeDtypeStruct(st.shape, F32)],
        name="conf_conv_decode",
    )(u, st, w, b, lg, lb)


def _layer_weights(l, p):
    qk_a = N_HEAD * HEAD_W
    sizes = (3 * qk_a, BRANCH_W, N_HEAD, N_HEAD, BRANCH_W, BRANCH_W, BRANCH_W, 2 * BRANCH_W, 3 * D_MODEL)
    offs = np.concatenate([[0], np.cumsum(sizes)])
    w_in = p["w_in"][l]
    col = lambda i: w_in[:, offs[i]:offs[i + 1]].astype(BF16)
    wba = jnp.pad(jnp.concatenate([w_in[:, offs[2]:offs[3]], w_in[:, offs[3]:offs[4]]], axis=1),
                  ((0, 0), (0, HEAD_W - 2 * N_HEAD))).astype(BF16)
    glu = col(7)
    lane_pad = lambda v: jnp.pad(v.reshape(1, N_HEAD), ((0, 0), (N_HEAD, HEAD_W - 2 * N_HEAD)))
    r2 = lambda v: v.reshape(1, -1)
    f1i, f1o = p["w_ffn1_in"][l].astype(BF16), p["w_ffn1_out"][l].astype(BF16)
    f2i, f2o = p["w_ffn2_in"][l].astype(BF16), p["w_ffn2_out"][l].astype(BF16)
    return dict(
        n1=p["norm_ffn1"][l], f1i=f1i, f1o=f1o,
        nm=r2(p["norm_mix"][l]),
        wqkv=col(0), wz=col(1), wba=wba, wq=col(4), wk=col(5), wv=col(6),
        wqt=col(4).T, wvt=col(6).T, dnc=p["diff_norm"][l].reshape(-1, 1),
        wga=glu[:, :BRANCH_W], wgb=glu[:, BRANCH_W:], wgate=col(8),
        cw=p["gdn_conv_w"][l], arow=lane_pad(p["gdn_a_log"][l]), dtrow=lane_pad(p["gdn_dt_bias"][l]),
        gn=r2(p["gdn_norm"][l]), lv=p["diff_lambda"][l], dn=r2(p["diff_norm"][l]),
        dw=p["conv_dw_w"][l], db=r2(p["conv_dw_b"][l]), lg=r2(p["conv_ln_g"][l]), lb=r2(p["conv_ln_b"][l]),
        wbr=p["w_branch"][l].astype(BF16), wo=p["w_out"][l].astype(BF16),
        n2=p["norm_ffn2"][l], f2i=f2i, f2o=f2o,
    )


def _prompt_layer(x, w, rope, n_seq, lam_init, layer, depth, kv_bufs, final):
    t_len = x.shape[0] // n_seq
    x = _ffn(x, w["n1"], w["f1i"], w["f1o"])
    qkv, z, ba = _proj_gdn(x, w["nm"], w["wqkv"], w["wz"], w["wba"])
    qt, k, k16, v, vt = _proj_attn(x, w["nm"], w["wqt"], w["wk"], w["wv"], w["wvt"], *rope, t_len,
                                   layer, depth, kv_bufs)
    u, gates = _proj_cg(x, w["nm"], w["wga"], w["wgb"], w["wgate"])
    o_a, s_new = _gdn(qkv, z, ba, w["cw"], w["arow"], w["dtrow"], w["gn"], n_seq)
    o_b = _attn(qt, k16, vt, w["lv"], w["dnc"], n_seq, lam_init)
    o_c = _conf(u, w["dw"], w["db"], w["lg"], w["lb"], n_seq)
    x = _merge_ffn(x, o_a, o_b, o_c, gates, w["wbr"], w["wo"], w["n2"], w["f2i"], w["f2o"], *final)
    gconv = qkv.reshape(n_seq, t_len, -1)[:, t_len - (CONV_A - 1):]
    cconv = u.reshape(n_seq, t_len, -1)[:, t_len - (CONV_C - 1):]
    return x, (k, v), s_new, gconv, cconv


def _sample_layer(x, w, cos, sin, cache_k, cache_v, page_flat, layer, s0, gst, cst, lam_init, final):
    n = x.shape[0]
    x = _ffn(x, w["n1"], w["f1i"], w["f1o"])
    qkv, z, ba = _proj_gdn(x, w["nm"], w["wqkv"], w["wz"], w["wba"])
    q, k, v = _proj_attn_dec(x, w["nm"], w["wq"], w["wk"], w["wv"], cos, sin)
    u, gates = _proj_cg(x, w["nm"], w["wga"], w["wgb"], w["wgate"])
    o_a, gst_new, s_new = _gdn_dec(qkv, z, ba, jnp.swapaxes(gst, 0, 1), s0,
                                   w["cw"], w["arow"], w["dtrow"], w["gn"])
    o_b = _attn_dec(q, k, v, cache_k, cache_v, page_flat, layer, w["lv"], w["dn"], lam_init)
    o_c, cst_new = _conf_dec(u, jnp.swapaxes(cst, 0, 1), w["dw"], w["db"], w["lg"], w["lb"])
    x = _merge_ffn(x, o_a.astype(BF16), o_b.reshape(n, BRANCH_W).astype(BF16), o_c.astype(BF16),
                   gates, w["wbr"], w["wo"], w["n2"], w["f2i"], w["f2o"], *final)
    return x, k, v, s_new, jnp.swapaxes(gst_new, 0, 1), jnp.swapaxes(cst_new, 0, 1)


def kernel(x_prompt, x_sample, cache_k, cache_v, page_table, state_gdn, state_gdn_conv, state_conv,
           norm_ffn1, w_ffn1_in, w_ffn1_out, norm_mix, w_in, gdn_conv_w, gdn_a_log, gdn_dt_bias,
           gdn_norm, diff_lambda, diff_norm, conv_dw_w, conv_dw_b, conv_ln_g, conv_ln_b,
           w_branch, w_out, norm_ffn2, w_ffn2_in, w_ffn2_out, norm_final):
    params = dict(norm_ffn1=norm_ffn1, w_ffn1_in=w_ffn1_in, w_ffn1_out=w_ffn1_out, norm_mix=norm_mix,
                  w_in=w_in, gdn_conv_w=gdn_conv_w, gdn_a_log=gdn_a_log, gdn_dt_bias=gdn_dt_bias,
                  gdn_norm=gdn_norm, diff_lambda=diff_lambda, diff_norm=diff_norm, conv_dw_w=conv_dw_w,
                  conv_dw_b=conv_dw_b, conv_ln_g=conv_ln_g, conv_ln_b=conv_ln_b, w_branch=w_branch,
                  w_out=w_out, norm_ffn2=norm_ffn2, w_ffn2_in=w_ffn2_in, w_ffn2_out=w_ffn2_out)
    depth = w_in.shape[0]
    bp, tp, d = x_prompt.shape
    bs, ts, _ = x_sample.shape
    past_len = page_table.shape[1] * PAGE
    n_pool = cache_k.shape[1]

    rope_p = _rope_table(0, tp)
    cos_s, sin_s, _, _ = _rope_table(past_len, 8)
    cos_s = jnp.broadcast_to(cos_s[0:1], (bs, HEAD_W))
    sin_s = jnp.broadcast_to(sin_s[0:1], (bs, HEAD_W))
    ck = cache_k.reshape(depth, n_pool, PAGE * N_HEAD, HEAD_W)
    cv = cache_v.reshape(depth, n_pool, PAGE * N_HEAD, HEAD_W)
    page_flat = page_table.reshape(-1)

    xp = x_prompt.reshape(bp * tp, d)
    xs = x_sample.reshape(bs * ts, d)
    outs = [[] for _ in range(8)]
    kv_p = None
    for l in range(depth):
        lam_init = 0.8 - 0.6 * math.exp(-0.3 * l)
        w = _layer_weights(l, params)
        final = (norm_final, l == depth - 1)
        xp, kv_p, s_p, gc_p, c_p = _prompt_layer(xp, w, rope_p, bp, lam_init, l, depth, kv_p, final)
        xs, k_s, v_s, s_s, gc_s, c_s = _sample_layer(xs, w, cos_s, sin_s, ck, cv, page_flat, l,
                                                     state_gdn[l], state_gdn_conv[l], state_conv[l], lam_init, final)
        vals = (k_s.reshape(bs, ts, N_HEAD, HEAD_W), v_s.reshape(bs, ts, N_HEAD, HEAD_W),
                s_p, s_s, gc_p, gc_s, c_p, c_s)
        for lst, val in zip(outs, vals):
            lst.append(val)
    y_prompt = xp.reshape(bp, tp, d)
    y_sample = xs.reshape(bs, ts, d)
    k_prompt, v_prompt = (a.reshape(depth, bp, tp, N_HEAD, HEAD_W) for a in kv_p)
    return (y_prompt, y_sample, k_prompt, v_prompt) + tuple(jnp.stack(o) for o in outs)
```

```python
import functools
import math

import jax
import jax.numpy as jnp
import numpy as np
from jax import lax
from jax.experimental import pallas as pl
from jax.experimental.pallas import tpu as pltpu

F32 = jnp.float32
BF16 = jnp.bfloat16

D_MODEL = 1024
N_HEAD = 4
HEAD_W = 128
HALF_W = 64
BRANCH_W = N_HEAD * HEAD_W
CONV_A = 4
CHUNK = 64
CONV_C = 31
D_FF = 2816
EPS = 1e-6
ROPE_THETA = 10000.0
PAGE = 128
NEG_INF = float("-inf")

VMEM_LIMIT = 56 * 1024 * 1024


def _cparams(sem, vmem=None):
    return pltpu.CompilerParams(dimension_semantics=sem, vmem_limit_bytes=vmem)


def _rms(x, g):
    return x * lax.rsqrt(jnp.mean(x * x, axis=-1, keepdims=True) + EPS) * g


def _sigmoid(x):
    return 0.5 * jnp.tanh(0.5 * x) + 0.5


def _silu(x):
    return x * _sigmoid(x)


def _softplus(x):
    return jnp.maximum(x, 0.0) + jnp.log1p(jnp.exp(-jnp.abs(x)))


def _row_tile(m, pref):
    return pref if m % pref == 0 else m


FF_CHUNK = 256


def _ffn_body(x, g, wi_ref, wo_ref):
    h = _rms(x, g).astype(BF16)
    acc = None
    for c in range(D_FF // FF_CHUNK):
        cols = slice(c * FF_CHUNK, (c + 1) * FF_CHUNK)
        gate = jnp.dot(h, wi_ref[:, cols], preferred_element_type=F32)
        up = jnp.dot(h, wi_ref[:, D_FF + c * FF_CHUNK:D_FF + (c + 1) * FF_CHUNK], preferred_element_type=F32)
        act = (_silu(gate) * up).astype(BF16)
        term = jnp.dot(act, wo_ref[cols, :], preferred_element_type=F32)
        acc = term if acc is None else acc + term
    return x + 0.5 * acc


def _ffn_kernel(x_ref, g_ref, wi_ref, wo_ref, o_ref):
    o_ref[...] = _ffn_body(x_ref[...], g_ref[...], wi_ref, wo_ref)


def _ffn(x, g, w_in, w_out, tm_pref=512):
    m, d = x.shape
    tm = _row_tile(m, tm_pref)
    resident = lambda a: pl.BlockSpec(a.shape, lambda i: (0, 0), pipeline_mode=pl.Buffered(1))
    return pl.pallas_call(
        _ffn_kernel,
        grid=(m // tm,),
        in_specs=[pl.BlockSpec((tm, d), lambda i: (i, 0)), pl.BlockSpec((1, d), lambda i: (0, 0)),
                  resident(w_in), resident(w_out)],
        out_specs=pl.BlockSpec((tm, d), lambda i: (i, 0)),
        out_shape=jax.ShapeDtypeStruct((m, d), F32),
        compiler_params=_cparams(("parallel",), VMEM_LIMIT),
        name="ffn",
    )(x, g.reshape(1, d), w_in, w_out)


def _proj_gdn_kernel(x_ref, g_ref, wqkv_ref, wz_ref, wba_ref, qkv_ref, z_ref, ba_ref):
    h = _rms(x_ref[...], g_ref[...]).astype(BF16)
    qkv_ref[...] = jnp.dot(h, wqkv_ref[...], preferred_element_type=F32)
    z_ref[...] = jnp.dot(h, wz_ref[...], preferred_element_type=F32)
    ba_ref[...] = jnp.dot(h, wba_ref[...], preferred_element_type=F32)


def _proj_gdn(x, g, wqkv, wz, wba, tm_pref=512):
    m, d = x.shape
    tm = _row_tile(m, tm_pref)
    row = lambda w: pl.BlockSpec((tm, w), lambda i: (i, 0))
    full = lambda a: pl.BlockSpec(a.shape, lambda i: (0, 0))
    return pl.pallas_call(
        _proj_gdn_kernel,
        grid=(m // tm,),
        in_specs=[row(d), full(g), full(wqkv), full(wz), full(wba)],
        out_specs=[row(wqkv.shape[1]), row(wz.shape[1]), row(wba.shape[1])],
        out_shape=[jax.ShapeDtypeStruct((m, w.shape[1]), F32) for w in (wqkv, wz, wba)],
        compiler_params=_cparams(("parallel",), VMEM_LIMIT),
        name="proj_gdn",
    )(x, g, wqkv, wz, wba)


def _rope_table_kernel(inv_ref, invc_ref, cos_ref, sin_ref, cost_ref, sint_ref, *, pos0, tm):
    i = pl.program_id(0)
    half = HALF_W // 2
    pos = lax.broadcasted_iota(jnp.int32, (tm, HEAD_W), 0) + (i * tm + pos0)
    ang = pos.astype(F32) * inv_ref[...]
    first = (lax.broadcasted_iota(jnp.int32, (tm, HEAD_W), 1) % HALF_W) < half
    s = jnp.sin(ang)
    cos_ref[...] = jnp.cos(ang)
    sin_ref[...] = jnp.where(first, -s, s)
    pos_t = lax.broadcasted_iota(jnp.int32, (HEAD_W, tm), 1) + (i * tm + pos0)
    ang_t = pos_t.astype(F32) * invc_ref[...]
    first_t = (lax.broadcasted_iota(jnp.int32, (HEAD_W, tm), 0) % HALF_W) < half
    s_t = jnp.sin(ang_t)
    cost_ref[...] = jnp.cos(ang_t)
    sint_ref[...] = jnp.where(first_t, -s_t, s_t)


def _rope_table(pos0, n):
    half = HALF_W // 2
    inv = ROPE_THETA ** (-jnp.arange(half, dtype=F32) / half)
    inv_rep = jnp.tile(inv, HEAD_W // half)
    tm = _row_tile(n, 512)
    return pl.pallas_call(
        functools.partial(_rope_table_kernel, pos0=pos0, tm=tm),
        grid=(n // tm,),
        in_specs=[pl.BlockSpec((1, HEAD_W), lambda i: (0, 0)), pl.BlockSpec((HEAD_W, 1), lambda i: (0, 0))],
        out_specs=[pl.BlockSpec((tm, HEAD_W), lambda i: (i, 0))] * 2
                  + [pl.BlockSpec((HEAD_W, tm), lambda i: (0, i))] * 2,
        out_shape=[jax.ShapeDtypeStruct((n, HEAD_W), F32)] * 2 + [jax.ShapeDtypeStruct((HEAD_W, n), F32)] * 2,
        compiler_params=_cparams(("parallel",)),
        name="rope_table",
    )(inv_rep.reshape(1, HEAD_W), inv_rep.reshape(HEAD_W, 1))


def _rope_rows(p, cos, sin, scale, out_refs):
    half = HALF_W // 2
    first = (lax.broadcasted_iota(jnp.int32, cos.shape, 1) % HALF_W) < half
    for hd in range(N_HEAD):
        xh = p[:, hd * HEAD_W:(hd + 1) * HEAD_W]
        partner = jnp.where(first, pltpu.roll(xh, HEAD_W - half, 1), pltpu.roll(xh, half, 1))
        y = (xh * cos + partner * sin) * scale
        for o in out_refs:
            if len(o.shape) == 3:
                o[:, hd, :] = y.astype(o.dtype)
            else:
                o[:, hd * HEAD_W:(hd + 1) * HEAD_W] = y.astype(o.dtype)


def _proj_attn_dec_kernel(x_ref, g_ref, wq_ref, wk_ref, wv_ref, cos_ref, sin_ref, q_ref, k_ref, v_ref):
    h = _rms(x_ref[...], g_ref[...]).astype(BF16)
    cos = cos_ref[...]
    sin = sin_ref[...]
    _rope_rows(jnp.dot(h, wq_ref[...], preferred_element_type=F32), cos, sin, HALF_W ** -0.5, (q_ref,))
    _rope_rows(jnp.dot(h, wk_ref[...], preferred_element_type=F32), cos, sin, 1.0, (k_ref,))
    v_ref[...] = jnp.dot(h, wv_ref[...], preferred_element_type=F32)


def _proj_attn_dec(x, g, wq, wk, wv, cos, sin):
    vm = pl.BlockSpec(memory_space=pltpu.VMEM)
    m = x.shape[0]
    return pl.pallas_call(
        _proj_attn_dec_kernel,
        in_specs=[vm] * 7,
        out_specs=[vm] * 3,
        out_shape=[jax.ShapeDtypeStruct((m, BRANCH_W), F32)] * 3,
        compiler_params=pltpu.CompilerParams(vmem_limit_bytes=VMEM_LIMIT),
        name="proj_attn_decode",
    )(x, g, wq, wk, wv, cos, sin)


Q_SCALE_LOG2 = (HALF_W ** -0.5) * math.log2(math.e)


def _proj_attn_kernel(x_ref, g_ref, wqt_ref, wk_ref, wv_ref, wvt_ref, cos_ref, sin_ref, cost_ref, sint_ref,
                      *rest):
    qt_ref, k_ref, k16_ref, v_ref, vt_ref = rest[-5:]
    h = _rms(x_ref[...], g_ref[...]).astype(BF16)
    nt = (((1,), (1,)), ((), ()))
    _rope_rows(jnp.dot(h, wk_ref[...], preferred_element_type=F32), cos_ref[...], sin_ref[...], 1.0,
               (k_ref, k16_ref))
    v = jnp.dot(h, wv_ref[...], preferred_element_type=F32)
    for hd in range(N_HEAD):
        v_ref[:, hd, :] = v[:, hd * HEAD_W:(hd + 1) * HEAD_W]
    vt_ref[...] = lax.dot_general(wvt_ref[...], h, nt, preferred_element_type=F32).astype(vt_ref.dtype)
    qt = lax.dot_general(wqt_ref[...], h, nt, preferred_element_type=F32)
    cos_t = cost_ref[...]
    sin_t = sint_ref[...]
    half = HALF_W // 2
    for hd in range(N_HEAD):
        xh = qt[hd * HEAD_W:(hd + 1) * HEAD_W]
        partner = jnp.concatenate([xh[half:2 * half], xh[0:half], xh[3 * half:4 * half], xh[2 * half:3 * half]],
                                  axis=0)
        qt_ref[hd * HEAD_W:(hd + 1) * HEAD_W, :] = ((xh * cos_t + partner * sin_t) * Q_SCALE_LOG2).astype(qt_ref.dtype)


def _proj_attn(x, g, wqt, wk, wv, wvt, cos, sin, cos_t, sin_t, rows_per_seq, layer, depth, kv_bufs, tm_pref=512):
    m, d = x.shape
    tm = _row_tile(rows_per_seq, tm_pref)
    n_t = rows_per_seq // tm
    row = lambda w: pl.BlockSpec((tm, w), lambda i: (i, 0))
    row3 = pl.BlockSpec((None, tm, N_HEAD, HEAD_W), lambda i: (layer, i, 0, 0))
    colb = pl.BlockSpec((BRANCH_W, tm), lambda i: (0, i))
    full = lambda a: pl.BlockSpec(a.shape, lambda i: (0, 0))
    tab = pl.BlockSpec((tm, HEAD_W), lambda i: (i % n_t, 0))
    tab_t = pl.BlockSpec((HEAD_W, tm), lambda i: (0, i % n_t))
    in_specs = [row(d), full(g), full(wqt), full(wk), full(wv), full(wvt), tab, tab, tab_t, tab_t]
    args = [x, g, wqt, wk, wv, wvt, cos, sin, cos_t, sin_t]
    aliases = {}
    if kv_bufs is not None:
        aliases = {len(args): 1, len(args) + 1: 3}
        in_specs += [pl.BlockSpec(memory_space=pl.ANY)] * 2
        args += list(kv_bufs)
    buf = jax.ShapeDtypeStruct((depth, m, N_HEAD, HEAD_W), F32)
    return pl.pallas_call(
        _proj_attn_kernel,
        grid=(m // tm,),
        in_specs=in_specs,
        out_specs=[colb, row3, row(BRANCH_W), row3, colb],
        out_shape=[jax.ShapeDtypeStruct((BRANCH_W, m), BF16), buf,
                   jax.ShapeDtypeStruct((m, BRANCH_W), BF16), buf,
                   jax.ShapeDtypeStruct((BRANCH_W, m), BF16)],
        input_output_aliases=aliases,
        compiler_params=_cparams(("parallel",), VMEM_LIMIT),
        name="proj_attn",
    )(*args)


def _proj_cg_kernel(x_ref, g_ref, wa_ref, wb_ref, wgate_ref, u_ref, gates_ref):
    h = _rms(x_ref[...], g_ref[...]).astype(BF16)
    a = jnp.dot(h, wa_ref[...], preferred_element_type=F32)
    b = jnp.dot(h, wb_ref[...], preferred_element_type=F32)
    u_ref[...] = a * _sigmoid(b)
    gates_ref[...] = _sigmoid(jnp.dot(h, wgate_ref[...], preferred_element_type=F32)).astype(gates_ref.dtype)


def _proj_cg(x, g, wa, wb, wgate, tm_pref=512):
    m, d = x.shape
    tm = _row_tile(m, tm_pref)
    row = lambda w: pl.BlockSpec((tm, w), lambda i: (i, 0))
    full = lambda a: pl.BlockSpec(a.shape, lambda i: (0, 0))
    return pl.pallas_call(
        _proj_cg_kernel,
        grid=(m // tm,),
        in_specs=[row(d), full(g), full(wa), full(wb), full(wgate)],
        out_specs=[row(wa.shape[1]), row(wgate.shape[1])],
        out_shape=[jax.ShapeDtypeStruct((m, wa.shape[1]), F32),
                   jax.ShapeDtypeStruct((m, wgate.shape[1]), BF16)],
        compiler_params=_cparams(("parallel",), VMEM_LIMIT),
        name="proj_cg",
    )(x, g, wa, wb, wgate)


def _l2norm(x):
    return x * lax.rsqrt(jnp.sum(x * x, axis=-1, keepdims=True) + EPS)


def _split2(x):
    hi = x.astype(BF16)
    return hi, (x - hi.astype(F32)).astype(BF16)


def _split3(x):
    hi = x.astype(BF16)
    r1 = x - hi.astype(F32)
    mid = r1.astype(BF16)
    return hi, mid, (r1 - mid.astype(F32)).astype(BF16)


def _dot3p(a, b):
    return (jnp.dot(a[0], b[0], preferred_element_type=F32) + jnp.dot(a[0], b[1], preferred_element_type=F32)
            + jnp.dot(a[1], b[0], preferred_element_type=F32))


def _gdn_gates(ba, arow, dtrow):
    beta = _sigmoid(ba)
    g = -jnp.exp(arow) * _softplus(ba + dtrow)
    return beta, g


def _gdn_kernel(qkv_ref, z_ref, ba_ref, cw_ref, arow_ref, dtrow_ref, gn_ref,
                o_ref, s_out_ref, ext_ref, s_ref, *, tm, nb):
    t = pl.program_id(0)
    nc = tm // CHUNK
    qkv_w = N_HEAD * HEAD_W

    @pl.when(t == 0)
    def _():
        ext_ref[:, 0:8, :] = jnp.zeros((nb, 8, ext_ref.shape[2]), F32)
        s_ref[...] = jnp.zeros(s_ref.shape, F32)

    r = lax.broadcasted_iota(jnp.int32, (tm, tm), 0)
    c = lax.broadcasted_iota(jnp.int32, (tm, tm), 1)
    same = r // CHUNK == c // CHUNK
    lbd = jnp.where(same, jnp.where(c <= r, 1.0, 0.0), 0.0).astype(BF16)
    qkv_l, beta_l, gc_l, gct_l = [], [], [], []
    for b in range(nb):
        ext_ref[b, 8:8 + tm, :] = qkv_ref[b]
        y = cw_ref[CONV_A - 1:CONV_A, :] * ext_ref[b, 8:8 + tm, :]
        for j in range(CONV_A - 1):
            off = 8 - (CONV_A - 1) + j
            y = y + cw_ref[j:j + 1, :] * ext_ref[b, off:off + tm, :]
        ext_ref[b, 0:8, :] = ext_ref[b, tm:tm + 8, :]
        qkv_l.append(_silu(y))
        beta, g_all = _gdn_gates(ba_ref[b], arow_ref[...], dtrow_ref[...])
        gc = None
        for piece in _split3(g_all):
            term = jnp.dot(lbd, piece, preferred_element_type=F32)
            gc = term if gc is None else gc + term
        beta_l.append(beta)
        gc_l.append(gc)
        gct_l.append(gc.T)

    ri = lax.broadcasted_iota(jnp.int32, (CHUNK, tm), 0)
    rc = lax.broadcasted_iota(jnp.int32, (CHUNK, tm), 1)
    eye_row = jnp.where(ri == rc % CHUNK, 1.0, 0.0)

    same16 = jnp.where(same, 1.0, 0.0).astype(BF16)

    def to_bd(parts):
        return tuple(jnp.concatenate([p] * nc, axis=0) * same16 for p in parts)

    def stack(a_parts, b_parts):
        return tuple(jnp.concatenate([a, b], axis=0) for a, b in zip(a_parts, b_parts))

    def from_bd(x_bd):
        acc = x_bd[0:CHUNK]
        for ch in range(1, nc):
            acc = acc + x_bd[ch * CHUNK:(ch + 1) * CHUNK]
        return acc

    gn = gn_ref[...]
    nt = (((1,), (1,)), ((), ()))
    units = [(b, hd) for b in range(nb) for hd in range(N_HEAD)]
    heads = range(len(units))
    qh = [_l2norm(qkv_l[b][:, hd * HEAD_W:(hd + 1) * HEAD_W]) * (HEAD_W ** -0.5) for b, hd in units]
    kh = [_l2norm(qkv_l[b][:, qkv_w + hd * HEAD_W:qkv_w + (hd + 1) * HEAD_W]) for b, hd in units]
    vh = [qkv_l[b][:, 2 * qkv_w + hd * HEAD_W:2 * qkv_w + (hd + 1) * HEAD_W] for b, hd in units]
    gcol = [gc_l[b][:, N_HEAD + hd:N_HEAD + hd + 1] for b, hd in units]
    grow = [gct_l[b][N_HEAD + hd:N_HEAD + hd + 1, :] for b, hd in units]
    bcol = [beta_l[b][:, hd:hd + 1] for b, hd in units]
    egc = [jnp.exp(g) for g in gcol]
    k16 = [k.astype(BF16) for k in kh]
    lower = jnp.logical_and(same, r >= c)
    decay = [jnp.exp(jnp.where(lower, gcol[hd] - grow[hd], NEG_INF)) for hd in heads]
    kk = [lax.dot_general(k16[hd], k16[hd], nt, preferred_element_type=F32) for hd in heads]
    qk = [lax.dot_general(qh[hd].astype(BF16), k16[hd], nt, preferred_element_type=F32) for hd in heads]
    n_bd = [jnp.where(r > c, -(kk[hd] * decay[hd] * bcol[hd]), 0.0) for hd in heads]
    attn16 = [(qk[hd] * decay[hd]).astype(BF16) for hd in heads]
    n_row = [from_bd(n) for n in n_bd]
    s_inv = [eye_row + n for n in n_row]
    x_parts = [_split2(n) for n in n_row]
    x_parts = [_split2(_dot3p(x_parts[hd], to_bd(x_parts[hd]))) for hd in heads]
    for _ in range(4):
        res = [_dot3p(stack(_split2(s_inv[hd]), x_parts[hd]), to_bd(x_parts[hd])) for hd in heads]
        s_inv = [s_inv[hd] + res[hd][0:CHUNK] for hd in heads]
        x_parts = [_split2(res[hd][CHUNK:2 * CHUNK]) for hd in heads]
    s_inv = [s_inv[hd] + _dot3p(_split2(s_inv[hd]), to_bd(x_parts[hd])) for hd in heads]
    sol = [_dot3p(to_bd(_split2(s_inv[hd])),
                  _split2(jnp.concatenate([vh[hd] * bcol[hd], kh[hd] * (bcol[hd] * egc[hd])], axis=1)))
           for hd in heads]
    qg16 = [(qh[hd] * egc[hd]).astype(BF16) for hd in heads]
    w16 = [sol[hd][:, HEAD_W:].astype(BF16) for hd in heads]
    s = [s_ref[b, hd] for b, hd in units]
    v_news = [[] for _ in heads]
    o_inters = [[] for _ in heads]
    for ch in range(nc):
        sl = slice(ch * CHUNK, (ch + 1) * CHUNK)
        for hd in heads:
            g_last = gcol[hd][ch * CHUNK + CHUNK - 1:(ch + 1) * CHUNK]
            res = jnp.dot(jnp.concatenate([w16[hd][sl], qg16[hd][sl]], axis=0), s[hd].astype(BF16),
                          preferred_element_type=F32)
            v_new = sol[hd][sl, :HEAD_W] - res[0:CHUNK]
            o_inters[hd].append(res[CHUNK:2 * CHUNK])
            v_news[hd].append(v_new)
            kd16 = (kh[hd][sl] * jnp.exp(g_last - gcol[hd][sl])).astype(BF16)
            s[hd] = s[hd] * jnp.exp(g_last) + lax.dot_general(kd16, v_new.astype(BF16), (((0,), (0,)), ((), ())),
                                                              preferred_element_type=F32)
    for u, (b, hd) in enumerate(units):
        s_ref[b, hd] = s[u]
        o = jnp.concatenate(o_inters[u], axis=0) + jnp.dot(
            attn16[u], jnp.concatenate(v_news[u], axis=0).astype(BF16), preferred_element_type=F32)
        zc = z_ref[b, :, hd * HEAD_W:(hd + 1) * HEAD_W]
        o_ref[b, :, hd * HEAD_W:(hd + 1) * HEAD_W] = (_rms(o, gn) * _silu(zc)).astype(o_ref.dtype)

    @pl.when(t == pl.num_programs(0) - 1)
    def _():
        s_out_ref[...] = s_ref[...]


def _gdn(qkv, z, ba, conv_w, arow, dtrow, gnorm, n_seq, tm_pref=256):
    m, w = qkv.shape
    t_len = m // n_seq
    tm = _row_tile(t_len, tm_pref)
    n_t = t_len // tm
    row = lambda wd: pl.BlockSpec((n_seq, tm, wd), lambda t: (0, t, 0))
    full = lambda a: pl.BlockSpec(a.shape, lambda t: (0, 0))
    seq3 = lambda a: a.reshape(n_seq, t_len, a.shape[1])
    state = (n_seq, N_HEAD, HEAD_W, HEAD_W)
    o_a, s_new = pl.pallas_call(
        functools.partial(_gdn_kernel, tm=tm, nb=n_seq),
        grid=(n_t,),
        in_specs=[row(w), row(BRANCH_W), row(HEAD_W), full(conv_w), full(arow), full(dtrow), full(gnorm)],
        out_specs=[row(BRANCH_W), pl.BlockSpec(state, lambda t: (0, 0, 0, 0))],
        out_shape=[jax.ShapeDtypeStruct((n_seq, t_len, BRANCH_W), BF16), jax.ShapeDtypeStruct(state, F32)],
        scratch_shapes=[pltpu.VMEM((n_seq, tm + 8, w), F32), pltpu.VMEM(state, F32)],
        compiler_params=_cparams(("arbitrary",), VMEM_LIMIT),
        name="gdn",
    )(seq3(qkv), seq3(z), seq3(ba), conv_w, arow, dtrow, gnorm)
    return o_a.reshape(m, BRANCH_W), s_new


def _lambda(lv, lam_init):
    s01 = jnp.sum(lv[0:1] * lv[1:2], axis=-1, keepdims=True)
    s23 = jnp.sum(lv[2:3] * lv[3:4], axis=-1, keepdims=True)
    return jnp.exp(s01) - jnp.exp(s23) + lam_init


KV_GROUP = 4


def _attn_kernel(qt_ref, k_ref, vt_ref, lv_ref, dn_ref, o_ref,
                 q1_ref, q2_ref, m1, l1, a1, m2, l2, a2, *, tq, lam_init):
    qi = pl.program_id(2)
    qt = qt_ref[...]
    chan = lax.broadcasted_iota(jnp.int32, qt.shape, 0)
    q1_ref[...] = jnp.where(chan < HALF_W, qt, jnp.zeros_like(qt))
    q2_ref[...] = jnp.where(chan >= HALF_W, qt, jnp.zeros_like(qt))
    maps = ((q1_ref, m1, l1, a1), (q2_ref, m2, l2, a2))
    for _, m, l, a in maps:
        m[...] = jnp.full(m.shape, NEG_INF, F32)
        l[...] = jnp.zeros(l.shape, F32)
        a[...] = jnp.zeros(a.shape, F32)

    def tiles(ki, n_tile, masked):
        offs = [pl.multiple_of((ki + j) * tq, tq) for j in range(n_tile)]
        k16 = [k_ref[pl.ds(off, tq), :] for off in offs]
        vt16 = [vt_ref[:, pl.ds(off, tq)] for off in offs]
        sts = [[jnp.dot(k, qr[...], preferred_element_type=F32) for k in k16] for qr, _, _, _ in maps]
        olds = [(m[...], l[...], a[...]) for _, m, l, a in maps]
        for st_list, (m_old, l_old, a_old), (_, m, l, a) in zip(sts, olds, maps):
            if masked:
                kv = lax.broadcasted_iota(jnp.int32, st_list[0].shape, 0)
                qq = lax.broadcasted_iota(jnp.int32, st_list[0].shape, 1)
                st_list = [jnp.where(kv <= qq, st, NEG_INF) for st in st_list]
            m_new = m_old
            for st in st_list:
                m_new = jnp.maximum(m_new, jnp.max(st, axis=0, keepdims=True))
            alpha = jnp.exp2(m_old - m_new)
            l_new = alpha * l_old
            a_new = alpha * a_old
            for st, vt in zip(st_list, vt16):
                p = jnp.exp2(st - m_new)
                l_new = l_new + jnp.sum(p, axis=0, keepdims=True)
                a_new = a_new + jnp.dot(vt, p.astype(BF16), preferred_element_type=F32)
            l[...] = l_new
            a[...] = a_new
            m[...] = m_new

    def body(kp, carry):
        tiles(KV_GROUP * kp, KV_GROUP, False)
        return carry

    n_group = qi // KV_GROUP
    lax.fori_loop(0, n_group, body, 0)
    done = n_group * KV_GROUP
    size = KV_GROUP // 2
    while size >= 1:
        take = ((qi - done) // size) > 0

        @pl.when(take)
        def _(done=done, size=size):
            tiles(done, size, False)

        done = done + jnp.where(take, size, 0)
        size //= 2
    tiles(qi, 1, True)

    lam = _lambda(lv_ref[...], lam_init)
    o = a1[...] / l1[...] - lam * (a2[...] / l2[...])
    o = o * lax.rsqrt(jnp.mean(o * o, axis=0, keepdims=True) + EPS) * dn_ref[...] * (1.0 - lam_init)
    o_ref[...] = o.T.astype(o_ref.dtype)


def _attn(qt, k16, vt, lv, dnorm_col, n_seq, lam_init, tq_pref=512):
    m = k16.shape[0]
    t_len = m // n_seq
    tq = _row_tile(t_len, tq_pref)
    n_q = t_len // tq
    full = lambda a: pl.BlockSpec(a.shape, lambda b, h, i: (0, 0))
    stat = pltpu.VMEM((1, tq), F32)
    accs = pltpu.VMEM((HEAD_W, tq), F32)
    return pl.pallas_call(
        functools.partial(_attn_kernel, tq=tq, lam_init=lam_init),
        grid=(n_seq, N_HEAD, n_q),
        in_specs=[pl.BlockSpec((HEAD_W, tq), lambda b, h, i: (h, b * n_q + i)),
                  pl.BlockSpec((t_len, HEAD_W), lambda b, h, i: (b, h)),
                  pl.BlockSpec((HEAD_W, t_len), lambda b, h, i: (h, b)),
                  full(lv), full(dnorm_col)],
        out_specs=pl.BlockSpec((tq, HEAD_W), lambda b, h, i: (b * n_q + i, h)),
        out_shape=jax.ShapeDtypeStruct((m, BRANCH_W), BF16),
        scratch_shapes=[pltpu.VMEM((HEAD_W, tq), BF16), pltpu.VMEM((HEAD_W, tq), BF16),
                        stat, stat, accs, stat, stat, accs],
        compiler_params=_cparams(("parallel", "parallel", "arbitrary"), VMEM_LIMIT),
        name="diff_attn",
    )(qt, k16, vt, lv, dnorm_col)


def _ln_swish(y, g, b):
    yc = y - jnp.mean(y, axis=-1, keepdims=True)
    yn = yc * lax.rsqrt(jnp.mean(yc * yc, axis=-1, keepdims=True) + EPS) * g + b
    return _silu(yn)


CARRY_C = 32


def _conf_kernel(u_ref, w_ref, b_ref, lg_ref, lb_ref, o_ref, ext_ref, shift_ref, *, tm):
    t = pl.program_id(1)

    @pl.when(t == 0)
    def _():
        ext_ref[0:CARRY_C, :] = jnp.zeros((CARRY_C, ext_ref.shape[1]), F32)

    ext_ref[CARRY_C:CARRY_C + tm, :] = u_ref[...]
    base = CARRY_C - (CONV_C - 1)
    acc = None
    for phase in range(8):
        starts = [base + j for j in range(CONV_C) if (base + j) % 8 == phase]
        span = max(starts) - phase + tm
        shift_ref[0:span, :] = ext_ref[phase:phase + span, :]
        for st in starts:
            term = w_ref[st - base:st - base + 1, :] * shift_ref[st - phase:st - phase + tm, :]
            acc = term if acc is None else acc + term
    ext_ref[0:CARRY_C, :] = ext_ref[tm:tm + CARRY_C, :]
    o_ref[...] = _ln_swish(acc + b_ref[...], lg_ref[...], lb_ref[...]).astype(o_ref.dtype)


def _conf(u, w, b, lg, lb, n_seq, tm_pref=256):
    m, d = u.shape
    t_len = m // n_seq
    tm = _row_tile(t_len, tm_pref)
    n_t = t_len // tm
    row = pl.BlockSpec((tm, d), lambda s, t: (s * n_t + t, 0))
    full = lambda a: pl.BlockSpec(a.shape, lambda s, t: (0, 0))
    return pl.pallas_call(
        functools.partial(_conf_kernel, tm=tm),
        grid=(n_seq, n_t),
        in_specs=[row, full(w), full(b), full(lg), full(lb)],
        out_specs=row,
        out_shape=jax.ShapeDtypeStruct((m, d), BF16),
        scratch_shapes=[pltpu.VMEM((tm + CARRY_C, d), F32), pltpu.VMEM((tm + CARRY_C, d), F32)],
        compiler_params=_cparams(("parallel", "arbitrary")),
        name="conf_conv",
    )(u, w, b, lg, lb)


def _merge_ffn_kernel(x_ref, oa_ref, ob_ref, oc_ref, gates_ref, wbr_ref, wom_ref, g_ref, wi_ref, wo_ref, gf_ref,
                      out_ref, *, final_norm):
    mix = None
    for b, o in enumerate((oa_ref, ob_ref, oc_ref)):
        pr = jnp.dot(o[...], wbr_ref[b], preferred_element_type=F32)
        term = gates_ref[:, b * D_MODEL:(b + 1) * D_MODEL] * pr
        mix = term if mix is None else mix + term
    x = x_ref[...] + jnp.dot(mix.astype(BF16), wom_ref[...], preferred_element_type=F32)
    y = _ffn_body(x, g_ref[...], wi_ref, wo_ref)
    out_ref[...] = _rms(y, gf_ref[...]) if final_norm else y


def _merge_ffn(x, oa, ob, oc, gates, wbr, wom, g, w_in, w_out, g_final, final_norm, tm_pref=512):
    m, d = x.shape
    tm = _row_tile(m, tm_pref)
    row = lambda w: pl.BlockSpec((tm, w), lambda i: (i, 0))
    vec = pl.BlockSpec((1, d), lambda i: (0, 0))
    resident = lambda a: pl.BlockSpec(a.shape, lambda i: (0,) * a.ndim, pipeline_mode=pl.Buffered(1))
    return pl.pallas_call(
        functools.partial(_merge_ffn_kernel, final_norm=final_norm),
        grid=(m // tm,),
        in_specs=[row(d), row(BRANCH_W), row(BRANCH_W), row(BRANCH_W), row(3 * d),
                  resident(wbr), resident(wom), vec, resident(w_in), resident(w_out), vec],
        out_specs=row(d),
        out_shape=jax.ShapeDtypeStruct((m, d), F32),
        compiler_params=_cparams(("parallel",), VMEM_LIMIT),
        name="merge_ffn",
    )(x, oa, ob, oc, gates, wbr, wom, g.reshape(1, d), w_in, w_out, g_final.reshape(1, d))


def _gdn_dec_kernel(qkv_ref, z_ref, ba_ref, st_ref, s0_ref, cw_ref, arow_ref, dtrow_ref, gn_ref,
                    o_ref, st_out_ref, s_out_ref, *, bt):
    qkv_w = N_HEAD * HEAD_W
    x = qkv_ref[...]
    y = cw_ref[CONV_A - 1:CONV_A, :] * x
    for j in range(CONV_A - 1):
        y = y + cw_ref[j:j + 1, :] * st_ref[j]
    for j in range(CONV_A - 2):
        st_out_ref[j] = st_ref[j + 1]
    st_out_ref[CONV_A - 2] = x
    qkv = _silu(y)
    beta, g_all = _gdn_gates(ba_ref[...], arow_ref[...], dtrow_ref[...])
    gn = gn_ref[...]
    pad = jnp.zeros((HEAD_W - bt, HEAD_W), F32)
    for hd in range(N_HEAD):
        qh = _l2norm(qkv[:, hd * HEAD_W:(hd + 1) * HEAD_W]) * (HEAD_W ** -0.5)
        kh = _l2norm(qkv[:, qkv_w + hd * HEAD_W:qkv_w + (hd + 1) * HEAD_W])
        vh = qkv[:, 2 * qkv_w + hd * HEAD_W:2 * qkv_w + (hd + 1) * HEAD_W]
        bcol = beta[:, hd:hd + 1]
        eg = jnp.exp(g_all[:, N_HEAD + hd:N_HEAD + hd + 1])
        qk = jnp.sum(qh * kh, axis=-1, keepdims=True)
        w_t = jnp.concatenate([kh * (bcol * eg), pad], axis=0).T
        q_t = jnp.concatenate([qh * eg, pad], axis=0).T
        k_t = jnp.concatenate([kh, pad], axis=0).T
        for b in range(bt):
            s = s0_ref[b, hd]
            ws = jnp.sum(w_t[:, b:b + 1] * s, axis=0, keepdims=True)
            qs = jnp.sum(q_t[:, b:b + 1] * s, axis=0, keepdims=True)
            v_new = vh[b:b + 1] * bcol[b:b + 1] - ws
            o = qs + qk[b:b + 1] * v_new
            s_out_ref[b, hd] = s * eg[b:b + 1] + k_t[:, b:b + 1] * v_new
            zc = z_ref[b:b + 1, hd * HEAD_W:(hd + 1) * HEAD_W]
            o_ref[b:b + 1, hd * HEAD_W:(hd + 1) * HEAD_W] = _rms(o, gn) * _silu(zc)


def _gdn_dec(qkv, z, ba, st, s0, conv_w, arow, dtrow, gnorm, bt=8):
    n, w = qkv.shape
    row = lambda wd: pl.BlockSpec((bt, wd), lambda i: (i, 0))
    full = lambda a: pl.BlockSpec(a.shape, lambda i: (0, 0))
    st_spec = pl.BlockSpec((CONV_A - 1, bt, w), lambda i: (0, i, 0))
    s_spec = pl.BlockSpec((bt, N_HEAD, HEAD_W, HEAD_W), lambda i: (i, 0, 0, 0))
    return pl.pallas_call(
        functools.partial(_gdn_dec_kernel, bt=bt),
        grid=(n // bt,),
        in_specs=[row(w), row(BRANCH_W), row(HEAD_W), st_spec, s_spec,
                  full(conv_w), full(arow), full(dtrow), full(gnorm)],
        out_specs=[row(BRANCH_W), st_spec, s_spec],
        out_shape=[jax.ShapeDtypeStruct((n, BRANCH_W), F32),
                   jax.ShapeDtypeStruct(st.shape, F32),
                   jax.ShapeDtypeStruct(s0.shape, F32)],
        compiler_params=_cparams(("parallel",)),
        name="gdn_decode",
    )(qkv, z, ba, st, s0, conv_w, arow, dtrow, gnorm)


def _attn_dec_kernel(pt_ref, q_ref, kn_ref, vn_ref, lv_ref, dn_ref, *rest, n_page, lam_init):
    k_refs = rest[:n_page]
    v_refs = rest[n_page:2 * n_page]
    o_ref, m_ref, l_ref, a_ref = rest[2 * n_page:]
    j = pl.program_id(1)

    @pl.when(j == 0)
    def _():
        m_ref[...] = jnp.full(m_ref.shape, NEG_INF, F32)
        l_ref[...] = jnp.zeros(l_ref.shape, F32)
        a_ref[...] = jnp.zeros(a_ref.shape, F32)

    q4 = q_ref[0]
    lane = lax.broadcasted_iota(jnp.int32, q4.shape, 1)
    qmat = jnp.concatenate([jnp.where(lane < HALF_W, q4, 0.0), jnp.where(lane >= HALF_W, q4, 0.0)], axis=0)
    qmat16 = qmat.astype(BF16)
    rows = PAGE * N_HEAD
    r = lax.broadcasted_iota(jnp.int32, (2 * N_HEAD, n_page * rows), 0)
    c = lax.broadcasted_iota(jnp.int32, (2 * N_HEAD, n_page * rows), 1)
    valid = (c % N_HEAD) == (r % N_HEAD)
    sc = jnp.concatenate(
        [lax.dot_general(qmat16, k_refs[p][...].astype(BF16), (((1,), (1,)), ((), ())),
                         preferred_element_type=F32) for p in range(n_page)], axis=1)
    sc = jnp.where(valid, sc, NEG_INF)
    m_old = m_ref[...]
    m_new = jnp.maximum(m_old, jnp.max(sc, axis=-1, keepdims=True))
    alpha = jnp.exp(m_old - m_new)
    pr = jnp.exp(sc - m_new)
    l_ref[...] = alpha * l_ref[...] + jnp.sum(pr, axis=-1, keepdims=True)
    pr = pr.astype(BF16)
    pv = None
    for p in range(n_page):
        term = jnp.dot(pr[:, p * rows:(p + 1) * rows], v_refs[p][...].astype(BF16), preferred_element_type=F32)
        pv = term if pv is None else pv + term
    a_ref[...] = alpha * a_ref[...] + pv
    m_ref[...] = m_new

    @pl.when(j == pl.num_programs(1) - 1)
    def _():
        k8 = jnp.concatenate([kn_ref[0], kn_ref[0]], axis=0)
        v8 = jnp.concatenate([vn_ref[0], vn_ref[0]], axis=0)
        s_self = jnp.sum(qmat * k8, axis=-1, keepdims=True)
        m_new = jnp.maximum(m_ref[...], s_self)
        alpha = jnp.exp(m_ref[...] - m_new)
        p_self = jnp.exp(s_self - m_new)
        l_fin = alpha * l_ref[...] + p_self
        a_fin = alpha * a_ref[...] + p_self * v8
        on = a_fin / l_fin
        lam = _lambda(lv_ref[...], lam_init)
        o = on[0:N_HEAD] - lam * on[N_HEAD:2 * N_HEAD]
        o_ref[0] = _rms(o, dn_ref[...]) * (1.0 - lam_init)


def _attn_dec(q, k_new, v_new, cache_k, cache_v, page_flat, layer, lv, dnorm, lam_init, n_page=16):
    n = q.shape[0]
    pages_per_seq = page_flat.shape[0] // n
    n_page = n_page if pages_per_seq % n_page == 0 else 1
    tok = pl.BlockSpec((1, N_HEAD, HEAD_W), lambda b, j, pt: (b, 0, 0))
    full = lambda a: pl.BlockSpec(a.shape, lambda b, j, pt: (0, 0))

    def page_spec(p):
        return pl.BlockSpec((None, None, PAGE * N_HEAD, HEAD_W),
                            lambda b, j, pt: (layer, pt[b * pages_per_seq + j * n_page + p], 0, 0))

    return pl.pallas_call(
        functools.partial(_attn_dec_kernel, n_page=n_page, lam_init=lam_init),
        grid_spec=pltpu.PrefetchScalarGridSpec(
            num_scalar_prefetch=1,
            grid=(n, pages_per_seq // n_page),
            in_specs=[tok, tok, tok, full(lv), full(dnorm)]
                     + [page_spec(p) for p in range(n_page)] * 2,
            out_specs=tok,
            scratch_shapes=[pltpu.VMEM((2 * N_HEAD, 1), F32), pltpu.VMEM((2 * N_HEAD, 1), F32),
                            pltpu.VMEM((2 * N_HEAD, HEAD_W), F32)],
        ),
        out_shape=jax.ShapeDtypeStruct((n, N_HEAD, HEAD_W), F32),
        compiler_params=_cparams(("parallel", "arbitrary")),
        name="diff_attn_decode",
    )(page_flat, q.reshape(n, N_HEAD, HEAD_W), k_new.reshape(n, N_HEAD, HEAD_W),
      v_new.reshape(n, N_HEAD, HEAD_W), lv, dnorm,
      *([cache_k] * n_page), *([cache_v] * n_page))


def _conf_dec_kernel(u_ref, st_ref, w_ref, b_ref, lg_ref, lb_ref, o_ref, st_out_ref):
    u = u_ref[...]
    acc = w_ref[CONV_C - 1:CONV_C, :] * u
    for j in range(CONV_C - 1):
        acc = acc + w_ref[j:j + 1, :] * st_ref[j]
    for j in range(CONV_C - 2):
        st_out_ref[j] = st_ref[j + 1]
    st_out_ref[CONV_C - 2] = u
    o_ref[...] = _ln_swish(acc + b_ref[...], lg_ref[...], lb_ref[...])


def _conf_dec(u, st, w, b, lg, lb):
    vm = pl.BlockSpec(memory_space=pltpu.VMEM)
    return pl.pallas_call(
        _conf_dec_kernel,
        in_specs=[vm] * 6,
        out_specs=[vm, vm],
        out_shape=[jax.ShapeDtypeStruct(u.shape, F32), jax.ShapeDtypeStruct(st.shape, F32)],
        name="conf_conv_decode",
    )(u, st, w, b, lg, lb)


def _layer_weights(l, p):
    qk_a = N_HEAD * HEAD_W
    sizes = (3 * qk_a, BRANCH_W, N_HEAD, N_HEAD, BRANCH_W, BRANCH_W, BRANCH_W, 2 * BRANCH_W, 3 * D_MODEL)
    offs = np.concatenate([[0], np.cumsum(sizes)])
    w_in = p["w_in"][l]
    col = lambda i: w_in[:, offs[i]:offs[i + 1]].astype(BF16)
    wba = jnp.pad(jnp.concatenate([w_in[:, offs[2]:offs[3]], w_in[:, offs[3]:offs[4]]], axis=1),
                  ((0, 0), (0, HEAD_W - 2 * N_HEAD))).astype(BF16)
    glu = col(7)
    lane_pad = lambda v: jnp.pad(v.reshape(1, N_HEAD), ((0, 0), (N_HEAD, HEAD_W - 2 * N_HEAD)))
    r2 = lambda v: v.reshape(1, -1)
    f1i, f1o = p["w_ffn1_in"][l].astype(BF16), p["w_ffn1_out"][l].astype(BF16)
    f2i, f2o = p["w_ffn2_in"][l].astype(BF16), p["w_ffn2_out"][l].astype(BF16)
    return dict(
        n1=p["norm_ffn1"][l], f1i=f1i, f1o=f1o,
        nm=r2(p["norm_mix"][l]),
        wqkv=col(0), wz=col(1), wba=wba, wq=col(4), wk=col(5), wv=col(6),
        wqt=col(4).T, wvt=col(6).T, dnc=p["diff_norm"][l].reshape(-1, 1),
        wga=glu[:, :BRANCH_W], wgb=glu[:, BRANCH_W:], wgate=col(8),
        cw=p["gdn_conv_w"][l], arow=lane_pad(p["gdn_a_log"][l]), dtrow=lane_pad(p["gdn_dt_bias"][l]),
        gn=r2(p["gdn_norm"][l]), lv=p["diff_lambda"][l], dn=r2(p["diff_norm"][l]),
        dw=p["conv_dw_w"][l], db=r2(p["conv_dw_b"][l]), lg=r2(p["conv_ln_g"][l]), lb=r2(p["conv_ln_b"][l]),
        wbr=p["w_branch"][l].astype(BF16), wo=p["w_out"][l].astype(BF16),
        n2=p["norm_ffn2"][l], f2i=f2i, f2o=f2o,
    )


def _prompt_layer(x, w, rope, n_seq, lam_init, layer, depth, kv_bufs, final):
    t_len = x.shape[0] // n_seq
    x = _ffn(x, w["n1"], w["f1i"], w["f1o"])
    qkv, z, ba = _proj_gdn(x, w["nm"], w["wqkv"], w["wz"], w["wba"])
    qt, k, k16, v, vt = _proj_attn(x, w["nm"], w["wqt"], w["wk"], w["wv"], w["wvt"], *rope, t_len,
                                   layer, depth, kv_bufs)
    u, gates = _proj_cg(x, w["nm"], w["wga"], w["wgb"], w["wgate"])
    o_a, s_new = _gdn(qkv, z, ba, w["cw"], w["arow"], w["dtrow"], w["gn"], n_seq)
    o_b = _attn(qt, k16, vt, w["lv"], w["dnc"], n_seq, lam_init)
    o_c = _conf(u, w["dw"], w["db"], w["lg"], w["lb"], n_seq)
    x = _merge_ffn(x, o_a, o_b, o_c, gates, w["wbr"], w["wo"], w["n2"], w["f2i"], w["f2o"], *final)
    gconv = qkv.reshape(n_seq, t_len, -1)[:, t_len - (CONV_A - 1):]
    cconv = u.reshape(n_seq, t_len, -1)[:, t_len - (CONV_C - 1):]
    return x, (k, v), s_new, gconv, cconv


def _sample_layer(x, w, cos, sin, cache_k, cache_v, page_flat, layer, s0, gst, cst, lam_init, final):
    n = x.shape[0]
    x = _ffn(x, w["n1"], w["f1i"], w["f1o"])
    qkv, z, ba = _proj_gdn(x, w["nm"], w["wqkv"], w["wz"], w["wba"])
    q, k, v = _proj_attn_dec(x, w["nm"], w["wq"], w["wk"], w["wv"], cos, sin)
    u, gates = _proj_cg(x, w["nm"], w["wga"], w["wgb"], w["wgate"])
    o_a, gst_new, s_new = _gdn_dec(qkv, z, ba, jnp.swapaxes(gst, 0, 1), s0,
                                   w["cw"], w["arow"], w["dtrow"], w["gn"])
    o_b = _attn_dec(q, k, v, cache_k, cache_v, page_flat, layer, w["lv"], w["dn"], lam_init)
    o_c, cst_new = _conf_dec(u, jnp.swapaxes(cst, 0, 1), w["dw"], w["db"], w["lg"], w["lb"])
    x = _merge_ffn(x, o_a.astype(BF16), o_b.reshape(n, BRANCH_W).astype(BF16), o_c.astype(BF16),
                   gates, w["wbr"], w["wo"], w["n2"], w["f2i"], w["f2o"], *final)
    return x, k, v, s_new, jnp.swapaxes(gst_new, 0, 1), jnp.swapaxes(cst_new, 0, 1)


def kernel(x_prompt, x_sample, cache_k, cache_v, page_table, state_gdn, state_gdn_conv, state_conv,
           norm_ffn1, w_ffn1_in, w_ffn1_out, norm_mix, w_in, gdn_conv_w, gdn_a_log, gdn_dt_bias,
           gdn_norm, diff_lambda, diff_norm, conv_dw_w, conv_dw_b, conv_ln_g, conv_ln_b,
           w_branch, w_out, norm_ffn2, w_ffn2_in, w_ffn2_out, norm_final):
    params = dict(norm_ffn1=norm_ffn1, w_ffn1_in=w_ffn1_in, w_ffn1_out=w_ffn1_out, norm_mix=norm_mix,
                  w_in=w_in, gdn_conv_w=gdn_conv_w, gdn_a_log=gdn_a_log, gdn_dt_bias=gdn_dt_bias,
                  gdn_norm=gdn_norm, diff_lambda=diff_lambda, diff_norm=diff_norm, conv_dw_w=conv_dw_w,
                  conv_dw_b=conv_dw_b, conv_ln_g=conv_ln_g, conv_ln_b=conv_ln_b, w_branch=w_branch,
                  w_out=w_out, norm_ffn2=norm_ffn2, w_ffn2_in=w_ffn2_in, w_ffn2_out=w_ffn2_out)
    depth = w_in.shape[0]
    bp, tp, d = x_prompt.shape
    bs, ts, _ = x_sample.shape
    past_len = page_table.shape[1] * PAGE
    n_pool = cache_k.shape[1]

    rope_p = _rope_table(0, tp)
    cos_s, sin_s, _, _ = _rope_table(past_len, 8)
    cos_s = jnp.broadcast_to(cos_s[0:1], (bs, HEAD_W))
    sin_s = jnp.broadcast_to(sin_s[0:1], (bs, HEAD_W))
    ck = cache_k.reshape(depth, n_pool, PAGE * N_HEAD, HEAD_W)
    cv = cache_v.reshape(depth, n_pool, PAGE * N_HEAD, HEAD_W)
    page_flat = page_table.reshape(-1)

    xp = x_prompt.reshape(bp * tp, d)
    xs = x_sample.reshape(bs * ts, d)
    outs = [[] for _ in range(8)]
    kv_p = None
    for l in range(depth):
        lam_init = 0.8 - 0.6 * math.exp(-0.3 * l)
        w = _layer_weights(l, params)
        final = (norm_final, l == depth - 1)
        xp, kv_p, s_p, gc_p, c_p = _prompt_layer(xp, w, rope_p, bp, lam_init, l, depth, kv_p, final)
        xs, k_s, v_s, s_s, gc_s, c_s = _sample_layer(xs, w, cos_s, sin_s, ck, cv, page_flat, l,
                                                     state_gdn[l], state_gdn_conv[l], state_conv[l], lam_init, final)
        vals = (k_s.reshape(bs, ts, N_HEAD, HEAD_W), v_s.reshape(bs, ts, N_HEAD, HEAD_W),
                s_p, s_s, gc_p, gc_s, c_p, c_s)
        for lst, val in zip(outs, vals):
            lst.append(val)
    y_prompt = xp.reshape(bp, tp, d)
    y_sample = xs.reshape(bs, ts, d)
    k_prompt, v_prompt = (a.reshape(depth, bp, tp, N_HEAD, HEAD_W) for a in kv_p)
    return (y_prompt, y_sample, k_prompt, v_prompt) + tuple(jnp.stack(o) for o in outs)
```

```python
import functools
import math

import jax
import jax.numpy as jnp
import numpy as np
from jax import lax
from jax.experimental import pallas as pl
from jax.experimental.pallas import tpu as pltpu

F32 = jnp.float32
BF16 = jnp.bfloat16

D_MODEL = 1024
N_HEAD = 4
HEAD_W = 128
HALF_W = 64
BRANCH_W = N_HEAD * HEAD_W
CONV_A = 4
CHUNK = 64
CONV_C = 31
D_FF = 2816
EPS = 1e-6
ROPE_THETA = 10000.0
PAGE = 128
NEG_INF = float("-inf")

VMEM_LIMIT = 56 * 1024 * 1024


def _cparams(sem, vmem=None):
    return pltpu.CompilerParams(dimension_semantics=sem, vmem_limit_bytes=vmem)


def _rms(x, g):
    return x * lax.rsqrt(jnp.mean(x * x, axis=-1, keepdims=True) + EPS) * g


def _sigmoid(x):
    return 0.5 * jnp.tanh(0.5 * x) + 0.5


def _silu(x):
    return x * _sigmoid(x)


def _softplus(x):
    return jnp.maximum(x, 0.0) + jnp.log1p(jnp.exp(-jnp.abs(x)))


def _row_tile(m, pref):
    return pref if m % pref == 0 else m


FF_CHUNK = 256


def _ffn_body(x, g, wi_ref, wo_ref):
    h = _rms(x, g).astype(BF16)
    acc = None
    for c in range(D_FF // FF_CHUNK):
        cols = slice(c * FF_CHUNK, (c + 1) * FF_CHUNK)
        gate = jnp.dot(h, wi_ref[:, cols], preferred_element_type=F32)
        up = jnp.dot(h, wi_ref[:, D_FF + c * FF_CHUNK:D_FF + (c + 1) * FF_CHUNK], preferred_element_type=F32)
        act = (_silu(gate) * up).astype(BF16)
        term = jnp.dot(act, wo_ref[cols, :], preferred_element_type=F32)
        acc = term if acc is None else acc + term
    return x + 0.5 * acc


def _ffn_kernel(x_ref, g_ref, wi_ref, wo_ref, o_ref):
    o_ref[...] = _ffn_body(x_ref[...], g_ref[...], wi_ref, wo_ref)


def _ffn(x, g, w_in, w_out, tm_pref=512):
    m, d = x.shape
    tm = _row_tile(m, tm_pref)
    resident = lambda a: pl.BlockSpec(a.shape, lambda i: (0, 0), pipeline_mode=pl.Buffered(1))
    return pl.pallas_call(
        _ffn_kernel,
        grid=(m // tm,),
        in_specs=[pl.BlockSpec((tm, d), lambda i: (i, 0)), pl.BlockSpec((1, d), lambda i: (0, 0)),
                  resident(w_in), resident(w_out)],
        out_specs=pl.BlockSpec((tm, d), lambda i: (i, 0)),
        out_shape=jax.ShapeDtypeStruct((m, d), F32),
        compiler_params=_cparams(("parallel",), VMEM_LIMIT),
        name="ffn",
    )(x, g.reshape(1, d), w_in, w_out)


def _proj_gdn_kernel(x_ref, g_ref, wqkv_ref, wz_ref, wba_ref, wa_ref, wb_ref, wgate_ref,
                     qkv_ref, z_ref, ba_ref, u_ref, gates_ref):
    h = _rms(x_ref[...], g_ref[...]).astype(BF16)
    qkv_ref[...] = jnp.dot(h, wqkv_ref[...], preferred_element_type=F32)
    z_ref[...] = jnp.dot(h, wz_ref[...], preferred_element_type=F32)
    ba_ref[...] = jnp.dot(h, wba_ref[...], preferred_element_type=F32)
    a = jnp.dot(h, wa_ref[...], preferred_element_type=F32)
    b = jnp.dot(h, wb_ref[...], preferred_element_type=F32)
    u_ref[...] = a * _sigmoid(b)
    gates_ref[...] = _sigmoid(jnp.dot(h, wgate_ref[...], preferred_element_type=F32)).astype(gates_ref.dtype)


def _proj_gdn(x, g, wqkv, wz, wba, wa, wb, wgate, tm_pref=512):
    m, d = x.shape
    tm = _row_tile(m, tm_pref)
    row = lambda w: pl.BlockSpec((tm, w), lambda i: (i, 0))
    full = lambda a: pl.BlockSpec(a.shape, lambda i: (0, 0), pipeline_mode=pl.Buffered(1))
    ws = (wqkv, wz, wba, wa, wb, wgate)
    outs = (wqkv, wz, wba, wa, wgate)
    return pl.pallas_call(
        _proj_gdn_kernel,
        grid=(m // tm,),
        in_specs=[row(d), pl.BlockSpec(g.shape, lambda i: (0, 0))] + [full(w) for w in ws],
        out_specs=[row(w.shape[1]) for w in outs],
        out_shape=[jax.ShapeDtypeStruct((m, w.shape[1]), BF16 if w is wgate else F32) for w in outs],
        compiler_params=_cparams(("parallel",), VMEM_LIMIT),
        name="proj_gdn_cg",
    )(x, g, *ws)


def _rope_table_kernel(inv_ref, invc_ref, cos_ref, sin_ref, cost_ref, sint_ref, *, pos0, tm):
    i = pl.program_id(0)
    half = HALF_W // 2
    pos = lax.broadcasted_iota(jnp.int32, (tm, HEAD_W), 0) + (i * tm + pos0)
    ang = pos.astype(F32) * inv_ref[...]
    first = (lax.broadcasted_iota(jnp.int32, (tm, HEAD_W), 1) % HALF_W) < half
    s = jnp.sin(ang)
    cos_ref[...] = jnp.cos(ang)
    sin_ref[...] = jnp.where(first, -s, s)
    pos_t = lax.broadcasted_iota(jnp.int32, (HEAD_W, tm), 1) + (i * tm + pos0)
    ang_t = pos_t.astype(F32) * invc_ref[...]
    first_t = (lax.broadcasted_iota(jnp.int32, (HEAD_W, tm), 0) % HALF_W) < half
    s_t = jnp.sin(ang_t)
    cost_ref[...] = jnp.cos(ang_t)
    sint_ref[...] = jnp.where(first_t, -s_t, s_t)


def _rope_table(pos0, n):
    half = HALF_W // 2
    inv = ROPE_THETA ** (-jnp.arange(half, dtype=F32) / half)
    inv_rep = jnp.tile(inv, HEAD_W // half)
    tm = _row_tile(n, 512)
    return pl.pallas_call(
        functools.partial(_rope_table_kernel, pos0=pos0, tm=tm),
        grid=(n // tm,),
        in_specs=[pl.BlockSpec((1, HEAD_W), lambda i: (0, 0)), pl.BlockSpec((HEAD_W, 1), lambda i: (0, 0))],
        out_specs=[pl.BlockSpec((tm, HEAD_W), lambda i: (i, 0))] * 2
                  + [pl.BlockSpec((HEAD_W, tm), lambda i: (0, i))] * 2,
        out_shape=[jax.ShapeDtypeStruct((n, HEAD_W), F32)] * 2 + [jax.ShapeDtypeStruct((HEAD_W, n), F32)] * 2,
        compiler_params=_cparams(("parallel",)),
        name="rope_table",
    )(inv_rep.reshape(1, HEAD_W), inv_rep.reshape(HEAD_W, 1))


def _rope_rows(p, cos, sin, scale, out_refs):
    half = HALF_W // 2
    first = (lax.broadcasted_iota(jnp.int32, cos.shape, 1) % HALF_W) < half
    for hd in range(N_HEAD):
        xh = p[:, hd * HEAD_W:(hd + 1) * HEAD_W]
        partner = jnp.where(first, pltpu.roll(xh, HEAD_W - half, 1), pltpu.roll(xh, half, 1))
        y = (xh * cos + partner * sin) * scale
        for o in out_refs:
            if len(o.shape) == 3:
                o[:, hd, :] = y.astype(o.dtype)
            else:
                o[:, hd * HEAD_W:(hd + 1) * HEAD_W] = y.astype(o.dtype)


def _proj_attn_dec_kernel(x_ref, g_ref, wq_ref, wk_ref, wv_ref, cos_ref, sin_ref, q_ref, k_ref, v_ref):
    h = _rms(x_ref[...], g_ref[...]).astype(BF16)
    cos = cos_ref[...]
    sin = sin_ref[...]
    _rope_rows(jnp.dot(h, wq_ref[...], preferred_element_type=F32), cos, sin, HALF_W ** -0.5, (q_ref,))
    _rope_rows(jnp.dot(h, wk_ref[...], preferred_element_type=F32), cos, sin, 1.0, (k_ref,))
    v_ref[...] = jnp.dot(h, wv_ref[...], preferred_element_type=F32)


def _proj_attn_dec(x, g, wq, wk, wv, cos, sin):
    vm = pl.BlockSpec(memory_space=pltpu.VMEM)
    m = x.shape[0]
    return pl.pallas_call(
        _proj_attn_dec_kernel,
        in_specs=[vm] * 7,
        out_specs=[vm] * 3,
        out_shape=[jax.ShapeDtypeStruct((m, BRANCH_W), F32)] * 3,
        compiler_params=pltpu.CompilerParams(vmem_limit_bytes=VMEM_LIMIT),
        name="proj_attn_decode",
    )(x, g, wq, wk, wv, cos, sin)


Q_SCALE_LOG2 = (HALF_W ** -0.5) * math.log2(math.e)


def _proj_attn_kernel(x_ref, g_ref, wqt_ref, wk_ref, wv_ref, wvt_ref, cos_ref, sin_ref, cost_ref, sint_ref,
                      *rest):
    qt_ref, k_ref, k16_ref, v_ref, vt_ref = rest[-5:]
    h = _rms(x_ref[...], g_ref[...]).astype(BF16)
    nt = (((1,), (1,)), ((), ()))
    _rope_rows(jnp.dot(h, wk_ref[...], preferred_element_type=F32), cos_ref[...], sin_ref[...], 1.0,
               (k_ref, k16_ref))
    v = jnp.dot(h, wv_ref[...], preferred_element_type=F32)
    for hd in range(N_HEAD):
        v_ref[:, hd, :] = v[:, hd * HEAD_W:(hd + 1) * HEAD_W]
    vt_ref[...] = lax.dot_general(wvt_ref[...], h, nt, preferred_element_type=F32).astype(vt_ref.dtype)
    qt = lax.dot_general(wqt_ref[...], h, nt, preferred_element_type=F32)
    cos_t = cost_ref[...]
    sin_t = sint_ref[...]
    half = HALF_W // 2
    for hd in range(N_HEAD):
        xh = qt[hd * HEAD_W:(hd + 1) * HEAD_W]
        partner = jnp.concatenate([xh[half:2 * half], xh[0:half], xh[3 * half:4 * half], xh[2 * half:3 * half]],
                                  axis=0)
        qt_ref[hd * HEAD_W:(hd + 1) * HEAD_W, :] = ((xh * cos_t + partner * sin_t) * Q_SCALE_LOG2).astype(qt_ref.dtype)


def _proj_attn(x, g, wqt, wk, wv, wvt, cos, sin, cos_t, sin_t, rows_per_seq, layer, depth, kv_bufs, tm_pref=512):
    m, d = x.shape
    tm = _row_tile(rows_per_seq, tm_pref)
    n_t = rows_per_seq // tm
    row = lambda w: pl.BlockSpec((tm, w), lambda i: (i, 0))
    row3 = pl.BlockSpec((None, tm, N_HEAD, HEAD_W), lambda i: (layer, i, 0, 0))
    colb = pl.BlockSpec((BRANCH_W, tm), lambda i: (0, i))
    full = lambda a: pl.BlockSpec(a.shape, lambda i: (0, 0))
    tab = pl.BlockSpec((tm, HEAD_W), lambda i: (i % n_t, 0))
    tab_t = pl.BlockSpec((HEAD_W, tm), lambda i: (0, i % n_t))
    in_specs = [row(d), full(g), full(wqt), full(wk), full(wv), full(wvt), tab, tab, tab_t, tab_t]
    args = [x, g, wqt, wk, wv, wvt, cos, sin, cos_t, sin_t]
    aliases = {}
    if kv_bufs is not None:
        aliases = {len(args): 1, len(args) + 1: 3}
        in_specs += [pl.BlockSpec(memory_space=pl.ANY)] * 2
        args += list(kv_bufs)
    buf = jax.ShapeDtypeStruct((depth, m, N_HEAD, HEAD_W), F32)
    return pl.pallas_call(
        _proj_attn_kernel,
        grid=(m // tm,),
        in_specs=in_specs,
        out_specs=[colb, row3, row(BRANCH_W), row3, colb],
        out_shape=[jax.ShapeDtypeStruct((BRANCH_W, m), BF16), buf,
                   jax.ShapeDtypeStruct((m, BRANCH_W), BF16), buf,
                   jax.ShapeDtypeStruct((BRANCH_W, m), BF16)],
        input_output_aliases=aliases,
        compiler_params=_cparams(("parallel",), VMEM_LIMIT),
        name="proj_attn",
    )(*args)


def _proj_cg_kernel(x_ref, g_ref, wa_ref, wb_ref, wgate_ref, u_ref, gates_ref):
    h = _rms(x_ref[...], g_ref[...]).astype(BF16)
    a = jnp.dot(h, wa_ref[...], preferred_element_type=F32)
    b = jnp.dot(h, wb_ref[...], preferred_element_type=F32)
    u_ref[...] = a * _sigmoid(b)
    gates_ref[...] = _sigmoid(jnp.dot(h, wgate_ref[...], preferred_element_type=F32)).astype(gates_ref.dtype)


def _proj_cg(x, g, wa, wb, wgate, tm_pref=512):
    m, d = x.shape
    tm = _row_tile(m, tm_pref)
    row = lambda w: pl.BlockSpec((tm, w), lambda i: (i, 0))
    full = lambda a: pl.BlockSpec(a.shape, lambda i: (0, 0))
    return pl.pallas_call(
        _proj_cg_kernel,
        grid=(m // tm,),
        in_specs=[row(d), full(g), full(wa), full(wb), full(wgate)],
        out_specs=[row(wa.shape[1]), row(wgate.shape[1])],
        out_shape=[jax.ShapeDtypeStruct((m, wa.shape[1]), F32),
                   jax.ShapeDtypeStruct((m, wgate.shape[1]), BF16)],
        compiler_params=_cparams(("parallel",), VMEM_LIMIT),
        name="proj_cg",
    )(x, g, wa, wb, wgate)


def _l2norm(x):
    return x * lax.rsqrt(jnp.sum(x * x, axis=-1, keepdims=True) + EPS)


def _split2(x):
    hi = x.astype(BF16)
    return hi, (x - hi.astype(F32)).astype(BF16)


def _split3(x):
    hi = x.astype(BF16)
    r1 = x - hi.astype(F32)
    mid = r1.astype(BF16)
    return hi, mid, (r1 - mid.astype(F32)).astype(BF16)


def _dot3p(a, b):
    return (jnp.dot(a[0], b[0], preferred_element_type=F32) + jnp.dot(a[0], b[1], preferred_element_type=F32)
            + jnp.dot(a[1], b[0], preferred_element_type=F32))


def _gdn_gates(ba, arow, dtrow):
    beta = _sigmoid(ba)
    g = -jnp.exp(arow) * _softplus(ba + dtrow)
    return beta, g


def _gdn_kernel(qkv_ref, z_ref, ba_ref, cw_ref, arow_ref, dtrow_ref, gn_ref,
                o_ref, s_out_ref, ext_ref, s_ref, *, tm, nb):
    t = pl.program_id(0)
    nc = tm // CHUNK
    qkv_w = N_HEAD * HEAD_W

    @pl.when(t == 0)
    def _():
        ext_ref[:, 0:8, :] = jnp.zeros((nb, 8, ext_ref.shape[2]), F32)
        s_ref[...] = jnp.zeros(s_ref.shape, F32)

    r = lax.broadcasted_iota(jnp.int32, (tm, tm), 0)
    c = lax.broadcasted_iota(jnp.int32, (tm, tm), 1)
    same = r // CHUNK == c // CHUNK
    lbd = jnp.where(same, jnp.where(c <= r, 1.0, 0.0), 0.0).astype(BF16)
    qkv_l, beta_l, gc_l, gct_l = [], [], [], []
    for b in range(nb):
        ext_ref[b, 8:8 + tm, :] = qkv_ref[b]
        y = cw_ref[CONV_A - 1:CONV_A, :] * ext_ref[b, 8:8 + tm, :]
        for j in range(CONV_A - 1):
            off = 8 - (CONV_A - 1) + j
            y = y + cw_ref[j:j + 1, :] * ext_ref[b, off:off + tm, :]
        ext_ref[b, 0:8, :] = ext_ref[b, tm:tm + 8, :]
        qkv_l.append(_silu(y))
        beta, g_all = _gdn_gates(ba_ref[b], arow_ref[...], dtrow_ref[...])
        gc = None
        for piece in _split3(g_all):
            term = jnp.dot(lbd, piece, preferred_element_type=F32)
            gc = term if gc is None else gc + term
        beta_l.append(beta)
        gc_l.append(gc)
        gct_l.append(gc.T)

    ri = lax.broadcasted_iota(jnp.int32, (CHUNK, tm), 0)
    rc = lax.broadcasted_iota(jnp.int32, (CHUNK, tm), 1)
    eye_row = jnp.where(ri == rc % CHUNK, 1.0, 0.0)

    same16 = jnp.where(same, 1.0, 0.0).astype(BF16)

    def to_bd(parts):
        return tuple(jnp.concatenate([p] * nc, axis=0) * same16 for p in parts)

    def stack(a_parts, b_parts):
        return tuple(jnp.concatenate([a, b], axis=0) for a, b in zip(a_parts, b_parts))

    def from_bd(x_bd):
        acc = x_bd[0:CHUNK]
        for ch in range(1, nc):
            acc = acc + x_bd[ch * CHUNK:(ch + 1) * CHUNK]
        return acc

    gn = gn_ref[...]
    nt = (((1,), (1,)), ((), ()))
    units = [(b, hd) for b in range(nb) for hd in range(N_HEAD)]
    heads = range(len(units))
    qh = [_l2norm(qkv_l[b][:, hd * HEAD_W:(hd + 1) * HEAD_W]) * (HEAD_W ** -0.5) for b, hd in units]
    kh = [_l2norm(qkv_l[b][:, qkv_w + hd * HEAD_W:qkv_w + (hd + 1) * HEAD_W]) for b, hd in units]
    vh = [qkv_l[b][:, 2 * qkv_w + hd * HEAD_W:2 * qkv_w + (hd + 1) * HEAD_W] for b, hd in units]
    gcol = [gc_l[b][:, N_HEAD + hd:N_HEAD + hd + 1] for b, hd in units]
    grow = [gct_l[b][N_HEAD + hd:N_HEAD + hd + 1, :] for b, hd in units]
    bcol = [beta_l[b][:, hd:hd + 1] for b, hd in units]
    egc = [jnp.exp(g) for g in gcol]
    k16 = [k.astype(BF16) for k in kh]
    lower = jnp.logical_and(same, r >= c)
    decay = [jnp.exp(jnp.where(lower, gcol[hd] - grow[hd], NEG_INF)) for hd in heads]
    kk = [lax.dot_general(k16[hd], k16[hd], nt, preferred_element_type=F32) for hd in heads]
    qk = [lax.dot_general(qh[hd].astype(BF16), k16[hd], nt, preferred_element_type=F32) for hd in heads]
    n_bd = [jnp.where(r > c, -(kk[hd] * decay[hd] * bcol[hd]), 0.0) for hd in heads]
    attn16 = [(qk[hd] * decay[hd]).astype(BF16) for hd in heads]
    n_row = [from_bd(n) for n in n_bd]
    s_inv = [eye_row + n for n in n_row]
    x_parts = [_split2(n) for n in n_row]
    x_parts = [_split2(_dot3p(x_parts[hd], to_bd(x_parts[hd]))) for hd in heads]
    for _ in range(4):
        res = [_dot3p(stack(_split2(s_inv[hd]), x_parts[hd]), to_bd(x_parts[hd])) for hd in heads]
        s_inv = [s_inv[hd] + res[hd][0:CHUNK] for hd in heads]
        x_parts = [_split2(res[hd][CHUNK:2 * CHUNK]) for hd in heads]
    s_inv = [s_inv[hd] + _dot3p(_split2(s_inv[hd]), to_bd(x_parts[hd])) for hd in heads]
    sol = [_dot3p(to_bd(_split2(s_inv[hd])),
                  _split2(jnp.concatenate([vh[hd] * bcol[hd], kh[hd] * (bcol[hd] * egc[hd])], axis=1)))
           for hd in heads]
    qg16 = [(qh[hd] * egc[hd]).astype(BF16) for hd in heads]
    w16 = [sol[hd][:, HEAD_W:].astype(BF16) for hd in heads]
    s = [s_ref[b, hd] for b, hd in units]
    v_news = [[] for _ in heads]
    o_inters = [[] for _ in heads]
    for ch in range(nc):
        sl = slice(ch * CHUNK, (ch + 1) * CHUNK)
        for hd in heads:
            g_last = gcol[hd][ch * CHUNK + CHUNK - 1:(ch + 1) * CHUNK]
            res = jnp.dot(jnp.concatenate([w16[hd][sl], qg16[hd][sl]], axis=0), s[hd].astype(BF16),
                          preferred_element_type=F32)
            v_new = sol[hd][sl, :HEAD_W] - res[0:CHUNK]
            o_inters[hd].append(res[CHUNK:2 * CHUNK])
            v_news[hd].append(v_new)
            kd16 = (kh[hd][sl] * jnp.exp(g_last - gcol[hd][sl])).astype(BF16)
            s[hd] = s[hd] * jnp.exp(g_last) + lax.dot_general(kd16, v_new.astype(BF16), (((0,), (0,)), ((), ())),
                                                              preferred_element_type=F32)
    for u, (b, hd) in enumerate(units):
        s_ref[b, hd] = s[u]
        o = jnp.concatenate(o_inters[u], axis=0) + jnp.dot(
            attn16[u], jnp.concatenate(v_news[u], axis=0).astype(BF16), preferred_element_type=F32)
        zc = z_ref[b, :, hd * HEAD_W:(hd + 1) * HEAD_W]
        o_ref[b, :, hd * HEAD_W:(hd + 1) * HEAD_W] = (_rms(o, gn) * _silu(zc)).astype(o_ref.dtype)

    @pl.when(t == pl.num_programs(0) - 1)
    def _():
        s_out_ref[...] = s_ref[...]


def _gdn(qkv, z, ba, conv_w, arow, dtrow, gnorm, n_seq, tm_pref=256):
    m, w = qkv.shape
    t_len = m // n_seq
    tm = _row_tile(t_len, tm_pref)
    n_t = t_len // tm
    row = lambda wd: pl.BlockSpec((n_seq, tm, wd), lambda t: (0, t, 0))
    full = lambda a: pl.BlockSpec(a.shape, lambda t: (0, 0))
    seq3 = lambda a: a.reshape(n_seq, t_len, a.shape[1])
    state = (n_seq, N_HEAD, HEAD_W, HEAD_W)
    o_a, s_new = pl.pallas_call(
        functools.partial(_gdn_kernel, tm=tm, nb=n_seq),
        grid=(n_t,),
        in_specs=[row(w), row(BRANCH_W), row(HEAD_W), full(conv_w), full(arow), full(dtrow), full(gnorm)],
        out_specs=[row(BRANCH_W), pl.BlockSpec(state, lambda t: (0, 0, 0, 0))],
        out_shape=[jax.ShapeDtypeStruct((n_seq, t_len, BRANCH_W), BF16), jax.ShapeDtypeStruct(state, F32)],
        scratch_shapes=[pltpu.VMEM((n_seq, tm + 8, w), F32), pltpu.VMEM(state, F32)],
        compiler_params=_cparams(("arbitrary",), VMEM_LIMIT),
        name="gdn",
    )(seq3(qkv), seq3(z), seq3(ba), conv_w, arow, dtrow, gnorm)
    return o_a.reshape(m, BRANCH_W), s_new


def _lambda(lv, lam_init):
    s01 = jnp.sum(lv[0:1] * lv[1:2], axis=-1, keepdims=True)
    s23 = jnp.sum(lv[2:3] * lv[3:4], axis=-1, keepdims=True)
    return jnp.exp(s01) - jnp.exp(s23) + lam_init


KV_GROUP = 4


def _attn_kernel(qt_ref, k_ref, vt_ref, lv_ref, dn_ref, o_ref,
                 q1_ref, q2_ref, m1, l1, a1, m2, l2, a2, *, tq, lam_init):
    qi = pl.program_id(2)
    qt = qt_ref[...]
    chan = lax.broadcasted_iota(jnp.int32, qt.shape, 0)
    q1_ref[...] = jnp.where(chan < HALF_W, qt, jnp.zeros_like(qt))
    q2_ref[...] = jnp.where(chan >= HALF_W, qt, jnp.zeros_like(qt))
    maps = ((q1_ref, m1, l1, a1), (q2_ref, m2, l2, a2))
    for _, m, l, a in maps:
        m[...] = jnp.full(m.shape, NEG_INF, F32)
        l[...] = jnp.zeros(l.shape, F32)
        a[...] = jnp.zeros(a.shape, F32)

    def tiles(ki, n_tile, masked):
        offs = [pl.multiple_of((ki + j) * tq, tq) for j in range(n_tile)]
        k16 = [k_ref[pl.ds(off, tq), :] for off in offs]
        vt16 = [vt_ref[:, pl.ds(off, tq)] for off in offs]
        sts = [[jnp.dot(k, qr[...], preferred_element_type=F32) for k in k16] for qr, _, _, _ in maps]
        olds = [(m[...], l[...], a[...]) for _, m, l, a in maps]
        for st_list, (m_old, l_old, a_old), (_, m, l, a) in zip(sts, olds, maps):
            if masked:
                kv = lax.broadcasted_iota(jnp.int32, st_list[0].shape, 0)
                qq = lax.broadcasted_iota(jnp.int32, st_list[0].shape, 1)
                st_list = [jnp.where(kv <= qq, st, NEG_INF) for st in st_list]
            m_new = m_old
            for st in st_list:
                m_new = jnp.maximum(m_new, jnp.max(st, axis=0, keepdims=True))
            alpha = jnp.exp2(m_old - m_new)
            l_new = alpha * l_old
            a_new = alpha * a_old
            for st, vt in zip(st_list, vt16):
                p = jnp.exp2(st - m_new)
                l_new = l_new + jnp.sum(p, axis=0, keepdims=True)
                a_new = a_new + jnp.dot(vt, p.astype(BF16), preferred_element_type=F32)
            l[...] = l_new
            a[...] = a_new
            m[...] = m_new

    def body(kp, carry):
        tiles(KV_GROUP * kp, KV_GROUP, False)
        return carry

    n_group = qi // KV_GROUP
    lax.fori_loop(0, n_group, body, 0)
    done = n_group * KV_GROUP
    size = KV_GROUP // 2
    while size >= 1:
        take = ((qi - done) // size) > 0

        @pl.when(take)
        def _(done=done, size=size):
            tiles(done, size, False)

        done = done + jnp.where(take, size, 0)
        size //= 2
    tiles(qi, 1, True)

    lam = _lambda(lv_ref[...], lam_init)
    o = a1[...] / l1[...] - lam * (a2[...] / l2[...])
    o = o * lax.rsqrt(jnp.mean(o * o, axis=0, keepdims=True) + EPS) * dn_ref[...] * (1.0 - lam_init)
    o_ref[...] = o.T.astype(o_ref.dtype)


def _attn(qt, k16, vt, lv, dnorm_col, n_seq, lam_init, tq_pref=512):
    m = k16.shape[0]
    t_len = m // n_seq
    tq = _row_tile(t_len, tq_pref)
    n_q = t_len // tq
    full = lambda a: pl.BlockSpec(a.shape, lambda b, h, i: (0, 0))
    stat = pltpu.VMEM((1, tq), F32)
    accs = pltpu.VMEM((HEAD_W, tq), F32)
    return pl.pallas_call(
        functools.partial(_attn_kernel, tq=tq, lam_init=lam_init),
        grid=(n_seq, N_HEAD, n_q),
        in_specs=[pl.BlockSpec((HEAD_W, tq), lambda b, h, i: (h, b * n_q + i)),
                  pl.BlockSpec((t_len, HEAD_W), lambda b, h, i: (b, h)),
                  pl.BlockSpec((HEAD_W, t_len), lambda b, h, i: (h, b)),
                  full(lv), full(dnorm_col)],
        out_specs=pl.BlockSpec((tq, HEAD_W), lambda b, h, i: (b * n_q + i, h)),
        out_shape=jax.ShapeDtypeStruct((m, BRANCH_W), BF16),
        scratch_shapes=[pltpu.VMEM((HEAD_W, tq), BF16), pltpu.VMEM((HEAD_W, tq), BF16),
                        stat, stat, accs, stat, stat, accs],
        compiler_params=_cparams(("parallel", "parallel", "arbitrary"), VMEM_LIMIT),
        name="diff_attn",
    )(qt, k16, vt, lv, dnorm_col)


def _ln_swish(y, g, b):
    yc = y - jnp.mean(y, axis=-1, keepdims=True)
    yn = yc * lax.rsqrt(jnp.mean(yc * yc, axis=-1, keepdims=True) + EPS) * g + b
    return _silu(yn)


CARRY_C = 32


def _conf_kernel(u_ref, w_ref, b_ref, lg_ref, lb_ref, o_ref, ext_ref, shift_ref, *, tm):
    t = pl.program_id(1)

    @pl.when(t == 0)
    def _():
        ext_ref[0:CARRY_C, :] = jnp.zeros((CARRY_C, ext_ref.shape[1]), F32)

    ext_ref[CARRY_C:CARRY_C + tm, :] = u_ref[...]
    base = CARRY_C - (CONV_C - 1)
    acc = None
    for phase in range(8):
        starts = [base + j for j in range(CONV_C) if (base + j) % 8 == phase]
        span = max(starts) - phase + tm
        shift_ref[0:span, :] = ext_ref[phase:phase + span, :]
        for st in starts:
            term = w_ref[st - base:st - base + 1, :] * shift_ref[st - phase:st - phase + tm, :]
            acc = term if acc is None else acc + term
    ext_ref[0:CARRY_C, :] = ext_ref[tm:tm + CARRY_C, :]
    o_ref[...] = _ln_swish(acc + b_ref[...], lg_ref[...], lb_ref[...]).astype(o_ref.dtype)


def _conf(u, w, b, lg, lb, n_seq, tm_pref=256):
    m, d = u.shape
    t_len = m // n_seq
    tm = _row_tile(t_len, tm_pref)
    n_t = t_len // tm
    row = pl.BlockSpec((tm, d), lambda s, t: (s * n_t + t, 0))
    full = lambda a: pl.BlockSpec(a.shape, lambda s, t: (0, 0))
    return pl.pallas_call(
        functools.partial(_conf_kernel, tm=tm),
        grid=(n_seq, n_t),
        in_specs=[row, full(w), full(b), full(lg), full(lb)],
        out_specs=row,
        out_shape=jax.ShapeDtypeStruct((m, d), BF16),
        scratch_shapes=[pltpu.VMEM((tm + CARRY_C, d), F32), pltpu.VMEM((tm + CARRY_C, d), F32)],
        compiler_params=_cparams(("parallel", "arbitrary")),
        name="conf_conv",
    )(u, w, b, lg, lb)


def _merge_ffn_kernel(x_ref, oa_ref, ob_ref, oc_ref, gates_ref, wbr_ref, wom_ref, g_ref, wi_ref, wo_ref, gf_ref,
                      out_ref, *, final_norm):
    mix = None
    for b, o in enumerate((oa_ref, ob_ref, oc_ref)):
        pr = jnp.dot(o[...], wbr_ref[b], preferred_element_type=F32)
        term = gates_ref[:, b * D_MODEL:(b + 1) * D_MODEL] * pr
        mix = term if mix is None else mix + term
    x = x_ref[...] + jnp.dot(mix.astype(BF16), wom_ref[...], preferred_element_type=F32)
    y = _ffn_body(x, g_ref[...], wi_ref, wo_ref)
    out_ref[...] = _rms(y, gf_ref[...]) if final_norm else y


def _merge_ffn(x, oa, ob, oc, gates, wbr, wom, g, w_in, w_out, g_final, final_norm, tm_pref=512):
    m, d = x.shape
    tm = _row_tile(m, tm_pref)
    row = lambda w: pl.BlockSpec((tm, w), lambda i: (i, 0))
    vec = pl.BlockSpec((1, d), lambda i: (0, 0))
    resident = lambda a: pl.BlockSpec(a.shape, lambda i: (0,) * a.ndim, pipeline_mode=pl.Buffered(1))
    return pl.pallas_call(
        functools.partial(_merge_ffn_kernel, final_norm=final_norm),
        grid=(m // tm,),
        in_specs=[row(d), row(BRANCH_W), row(BRANCH_W), row(BRANCH_W), row(3 * d),
                  resident(wbr), resident(wom), vec, resident(w_in), resident(w_out), vec],
        out_specs=row(d),
        out_shape=jax.ShapeDtypeStruct((m, d), F32),
        compiler_params=_cparams(("parallel",), VMEM_LIMIT),
        name="merge_ffn",
    )(x, oa, ob, oc, gates, wbr, wom, g.reshape(1, d), w_in, w_out, g_final.reshape(1, d))


def _gdn_dec_kernel(qkv_ref, z_ref, ba_ref, st_ref, s0_ref, cw_ref, arow_ref, dtrow_ref, gn_ref,
                    o_ref, st_out_ref, s_out_ref, *, bt):
    qkv_w = N_HEAD * HEAD_W
    x = qkv_ref[...]
    y = cw_ref[CONV_A - 1:CONV_A, :] * x
    for j in range(CONV_A - 1):
        y = y + cw_ref[j:j + 1, :] * st_ref[j]
    for j in range(CONV_A - 2):
        st_out_ref[j] = st_ref[j + 1]
    st_out_ref[CONV_A - 2] = x
    qkv = _silu(y)
    beta, g_all = _gdn_gates(ba_ref[...], arow_ref[...], dtrow_ref[...])
    gn = gn_ref[...]
    pad = jnp.zeros((HEAD_W - bt, HEAD_W), F32)
    for hd in range(N_HEAD):
        qh = _l2norm(qkv[:, hd * HEAD_W:(hd + 1) * HEAD_W]) * (HEAD_W ** -0.5)
        kh = _l2norm(qkv[:, qkv_w + hd * HEAD_W:qkv_w + (hd + 1) * HEAD_W])
        vh = qkv[:, 2 * qkv_w + hd * HEAD_W:2 * qkv_w + (hd + 1) * HEAD_W]
        bcol = beta[:, hd:hd + 1]
        eg = jnp.exp(g_all[:, N_HEAD + hd:N_HEAD + hd + 1])
        qk = jnp.sum(qh * kh, axis=-1, keepdims=True)
        w_t = jnp.concatenate([kh * (bcol * eg), pad], axis=0).T
        q_t = jnp.concatenate([qh * eg, pad], axis=0).T
        k_t = jnp.concatenate([kh, pad], axis=0).T
        for b in range(bt):
            s = s0_ref[b, hd]
            ws = jnp.sum(w_t[:, b:b + 1] * s, axis=0, keepdims=True)
            qs = jnp.sum(q_t[:, b:b + 1] * s, axis=0, keepdims=True)
            v_new = vh[b:b + 1] * bcol[b:b + 1] - ws
            o = qs + qk[b:b + 1] * v_new
            s_out_ref[b, hd] = s * eg[b:b + 1] + k_t[:, b:b + 1] * v_new
            zc = z_ref[b:b + 1, hd * HEAD_W:(hd + 1) * HEAD_W]
            o_ref[b:b + 1, hd * HEAD_W:(hd + 1) * HEAD_W] = _rms(o, gn) * _silu(zc)


def _gdn_dec(qkv, z, ba, st, s0, conv_w, arow, dtrow, gnorm, bt=8):
    n, w = qkv.shape
    row = lambda wd: pl.BlockSpec((bt, wd), lambda i: (i, 0))
    full = lambda a: pl.BlockSpec(a.shape, lambda i: (0, 0))
    st_spec = pl.BlockSpec((CONV_A - 1, bt, w), lambda i: (0, i, 0))
    s_spec = pl.BlockSpec((bt, N_HEAD, HEAD_W, HEAD_W), lambda i: (i, 0, 0, 0))
    return pl.pallas_call(
        functools.partial(_gdn_dec_kernel, bt=bt),
        grid=(n // bt,),
        in_specs=[row(w), row(BRANCH_W), row(HEAD_W), st_spec, s_spec,
                  full(conv_w), full(arow), full(dtrow), full(gnorm)],
        out_specs=[row(BRANCH_W), st_spec, s_spec],
        out_shape=[jax.ShapeDtypeStruct((n, BRANCH_W), F32),
                   jax.ShapeDtypeStruct(st.shape, F32),
                   jax.ShapeDtypeStruct(s0.shape, F32)],
        compiler_params=_cparams(("parallel",)),
        name="gdn_decode",
    )(qkv, z, ba, st, s0, conv_w, arow, dtrow, gnorm)


def _attn_dec_kernel(pt_ref, q_ref, kn_ref, vn_ref, lv_ref, dn_ref, *rest, n_page, lam_init):
    k_refs = rest[:n_page]
    v_refs = rest[n_page:2 * n_page]
    o_ref, m_ref, l_ref, a_ref = rest[2 * n_page:]
    j = pl.program_id(1)

    @pl.when(j == 0)
    def _():
        m_ref[...] = jnp.full(m_ref.shape, NEG_INF, F32)
        l_ref[...] = jnp.zeros(l_ref.shape, F32)
        a_ref[...] = jnp.zeros(a_ref.shape, F32)

    q4 = q_ref[0]
    lane = lax.broadcasted_iota(jnp.int32, q4.shape, 1)
    qmat = jnp.concatenate([jnp.where(lane < HALF_W, q4, 0.0), jnp.where(lane >= HALF_W, q4, 0.0)], axis=0)
    qmat16 = qmat.astype(BF16)
    rows = PAGE * N_HEAD
    r = lax.broadcasted_iota(jnp.int32, (2 * N_HEAD, n_page * rows), 0)
    c = lax.broadcasted_iota(jnp.int32, (2 * N_HEAD, n_page * rows), 1)
    valid = (c % N_HEAD) == (r % N_HEAD)
    sc = jnp.concatenate(
        [lax.dot_general(qmat16, k_refs[p][...].astype(BF16), (((1,), (1,)), ((), ())),
                         preferred_element_type=F32) for p in range(n_page)], axis=1)
    sc = jnp.where(valid, sc, NEG_INF)
    m_old = m_ref[...]
    m_new = jnp.maximum(m_old, jnp.max(sc, axis=-1, keepdims=True))
    alpha = jnp.exp(m_old - m_new)
    pr = jnp.exp(sc - m_new)
    l_ref[...] = alpha * l_ref[...] + jnp.sum(pr, axis=-1, keepdims=True)
    pr = pr.astype(BF16)
    pv = None
    for p in range(n_page):
        term = jnp.dot(pr[:, p * rows:(p + 1) * rows], v_refs[p][...].astype(BF16), preferred_element_type=F32)
        pv = term if pv is None else pv + term
    a_ref[...] = alpha * a_ref[...] + pv
    m_ref[...] = m_new

    @pl.when(j == pl.num_programs(1) - 1)
    def _():
        k8 = jnp.concatenate([kn_ref[0], kn_ref[0]], axis=0)
        v8 = jnp.concatenate([vn_ref[0], vn_ref[0]], axis=0)
        s_self = jnp.sum(qmat * k8, axis=-1, keepdims=True)
        m_new = jnp.maximum(m_ref[...], s_self)
        alpha = jnp.exp(m_ref[...] - m_new)
        p_self = jnp.exp(s_self - m_new)
        l_fin = alpha * l_ref[...] + p_self
        a_fin = alpha * a_ref[...] + p_self * v8
        on = a_fin / l_fin
        lam = _lambda(lv_ref[...], lam_init)
        o = on[0:N_HEAD] - lam * on[N_HEAD:2 * N_HEAD]
        o_ref[0] = _rms(o, dn_ref[...]) * (1.0 - lam_init)


def _attn_dec(q, k_new, v_new, cache_k, cache_v, page_flat, layer, lv, dnorm, lam_init, n_page=16):
    n = q.shape[0]
    pages_per_seq = page_flat.shape[0] // n
    n_page = n_page if pages_per_seq % n_page == 0 else 1
    tok = pl.BlockSpec((1, N_HEAD, HEAD_W), lambda b, j, pt: (b, 0, 0))
    full = lambda a: pl.BlockSpec(a.shape, lambda b, j, pt: (0, 0))

    def page_spec(p):
        return pl.BlockSpec((None, None, PAGE * N_HEAD, HEAD_W),
                            lambda b, j, pt: (layer, pt[b * pages_per_seq + j * n_page + p], 0, 0))

    return pl.pallas_call(
        functools.partial(_attn_dec_kernel, n_page=n_page, lam_init=lam_init),
        grid_spec=pltpu.PrefetchScalarGridSpec(
            num_scalar_prefetch=1,
            grid=(n, pages_per_seq // n_page),
            in_specs=[tok, tok, tok, full(lv), full(dnorm)]
                     + [page_spec(p) for p in range(n_page)] * 2,
            out_specs=tok,
            scratch_shapes=[pltpu.VMEM((2 * N_HEAD, 1), F32), pltpu.VMEM((2 * N_HEAD, 1), F32),
                            pltpu.VMEM((2 * N_HEAD, HEAD_W), F32)],
        ),
        out_shape=jax.ShapeDtypeStruct((n, N_HEAD, HEAD_W), F32),
        compiler_params=_cparams(("parallel", "arbitrary")),
        name="diff_attn_decode",
    )(page_flat, q.reshape(n, N_HEAD, HEAD_W), k_new.reshape(n, N_HEAD, HEAD_W),
      v_new.reshape(n, N_HEAD, HEAD_W), lv, dnorm,
      *([cache_k] * n_page), *([cache_v] * n_page))


def _conf_dec_kernel(u_ref, st_ref, w_ref, b_ref, lg_ref, lb_ref, o_ref, st_out_ref):
    u = u_ref[...]
    acc = w_ref[CONV_C - 1:CONV_C, :] * u
    for j in range(CONV_C - 1):
        acc = acc + w_ref[j:j + 1, :] * st_ref[j]
    for j in range(CONV_C - 2):
        st_out_ref[j] = st_ref[j + 1]
    st_out_ref[CONV_C - 2] = u
    o_ref[...] = _ln_swish(acc + b_ref[...], lg_ref[...], lb_ref[...])


def _conf_dec(u, st, w, b, lg, lb):
    vm = pl.BlockSpec(memory_space=pltpu.VMEM)
    return pl.pallas_call(
        _conf_dec_kernel,
        in_specs=[vm] * 6,
        out_specs=[vm, vm],
        out_shape=[jax.ShapeDtypeStruct(u.shape, F32), jax.ShapeDtypeStruct(st.shape, F32)],
        name="conf_conv_decode",
    )(u, st, w, b, lg, lb)


def _layer_weights(l, p):
    qk_a = N_HEAD * HEAD_W
    sizes = (3 * qk_a, BRANCH_W, N_HEAD, N_HEAD, BRANCH_W, BRANCH_W, BRANCH_W, 2 * BRANCH_W, 3 * D_MODEL)
    offs = np.concatenate([[0], np.cumsum(sizes)])
    w_in = p["w_in"][l]
    col = lambda i: w_in[:, offs[i]:offs[i + 1]].astype(BF16)
    wba = jnp.pad(jnp.concatenate([w_in[:, offs[2]:offs[3]], w_in[:, offs[3]:offs[4]]], axis=1),
                  ((0, 0), (0, HEAD_W - 2 * N_HEAD))).astype(BF16)
    glu = col(7)
    lane_pad = lambda v: jnp.pad(v.reshape(1, N_HEAD), ((0, 0), (N_HEAD, HEAD_W - 2 * N_HEAD)))
    r2 = lambda v: v.reshape(1, -1)
    f1i, f1o = p["w_ffn1_in"][l].astype(BF16), p["w_ffn1_out"][l].astype(BF16)
    f2i, f2o = p["w_ffn2_in"][l].astype(BF16), p["w_ffn2_out"][l].astype(BF16)
    return dict(
        n1=p["norm_ffn1"][l], f1i=f1i, f1o=f1o,
        nm=r2(p["norm_mix"][l]),
        wqkv=col(0), wz=col(1), wba=wba, wq=col(4), wk=col(5), wv=col(6),
        wqt=col(4).T, wvt=col(6).T, dnc=p["diff_norm"][l].reshape(-1, 1),
        wga=glu[:, :BRANCH_W], wgb=glu[:, BRANCH_W:], wgate=col(8),
        cw=p["gdn_conv_w"][l], arow=lane_pad(p["gdn_a_log"][l]), dtrow=lane_pad(p["gdn_dt_bias"][l]),
        gn=r2(p["gdn_norm"][l]), lv=p["diff_lambda"][l], dn=r2(p["diff_norm"][l]),
        dw=p["conv_dw_w"][l], db=r2(p["conv_dw_b"][l]), lg=r2(p["conv_ln_g"][l]), lb=r2(p["conv_ln_b"][l]),
        wbr=p["w_branch"][l].astype(BF16), wo=p["w_out"][l].astype(BF16),
        n2=p["norm_ffn2"][l], f2i=f2i, f2o=f2o,
    )


def _prompt_layer(x, w, rope, n_seq, lam_init, layer, depth, kv_bufs, final):
    t_len = x.shape[0] // n_seq
    x = _ffn(x, w["n1"], w["f1i"], w["f1o"])
    qkv, z, ba, u, gates = _proj_gdn(x, w["nm"], w["wqkv"], w["wz"], w["wba"], w["wga"], w["wgb"], w["wgate"])
    qt, k, k16, v, vt = _proj_attn(x, w["nm"], w["wqt"], w["wk"], w["wv"], w["wvt"], *rope, t_len,
                                   layer, depth, kv_bufs)
    o_a, s_new = _gdn(qkv, z, ba, w["cw"], w["arow"], w["dtrow"], w["gn"], n_seq)
    o_b = _attn(qt, k16, vt, w["lv"], w["dnc"], n_seq, lam_init)
    o_c = _conf(u, w["dw"], w["db"], w["lg"], w["lb"], n_seq)
    x = _merge_ffn(x, o_a, o_b, o_c, gates, w["wbr"], w["wo"], w["n2"], w["f2i"], w["f2o"], *final)
    gconv = qkv.reshape(n_seq, t_len, -1)[:, t_len - (CONV_A - 1):]
    cconv = u.reshape(n_seq, t_len, -1)[:, t_len - (CONV_C - 1):]
    return x, (k, v), s_new, gconv, cconv


def _sample_layer(x, w, cos, sin, cache_k, cache_v, page_flat, layer, s0, gst, cst, lam_init, final):
    n = x.shape[0]
    x = _ffn(x, w["n1"], w["f1i"], w["f1o"])
    qkv, z, ba, u, gates = _proj_gdn(x, w["nm"], w["wqkv"], w["wz"], w["wba"], w["wga"], w["wgb"], w["wgate"])
    q, k, v = _proj_attn_dec(x, w["nm"], w["wq"], w["wk"], w["wv"], cos, sin)
    o_a, gst_new, s_new = _gdn_dec(qkv, z, ba, jnp.swapaxes(gst, 0, 1), s0,
                                   w["cw"], w["arow"], w["dtrow"], w["gn"])
    o_b = _attn_dec(q, k, v, cache_k, cache_v, page_flat, layer, w["lv"], w["dn"], lam_init)
    o_c, cst_new = _conf_dec(u, jnp.swapaxes(cst, 0, 1), w["dw"], w["db"], w["lg"], w["lb"])
    x = _merge_ffn(x, o_a.astype(BF16), o_b.reshape(n, BRANCH_W).astype(BF16), o_c.astype(BF16),
                   gates, w["wbr"], w["wo"], w["n2"], w["f2i"], w["f2o"], *final)
    return x, k, v, s_new, jnp.swapaxes(gst_new, 0, 1), jnp.swapaxes(cst_new, 0, 1)


def kernel(x_prompt, x_sample, cache_k, cache_v, page_table, state_gdn, state_gdn_conv, state_conv,
           norm_ffn1, w_ffn1_in, w_ffn1_out, norm_mix, w_in, gdn_conv_w, gdn_a_log, gdn_dt_bias,
           gdn_norm, diff_lambda, diff_norm, conv_dw_w, conv_dw_b, conv_ln_g, conv_ln_b,
           w_branch, w_out, norm_ffn2, w_ffn2_in, w_ffn2_out, norm_final):
    params = dict(norm_ffn1=norm_ffn1, w_ffn1_in=w_ffn1_in, w_ffn1_out=w_ffn1_out, norm_mix=norm_mix,
                  w_in=w_in, gdn_conv_w=gdn_conv_w, gdn_a_log=gdn_a_log, gdn_dt_bias=gdn_dt_bias,
                  gdn_norm=gdn_norm, diff_lambda=diff_lambda, diff_norm=diff_norm, conv_dw_w=conv_dw_w,
                  conv_dw_b=conv_dw_b, conv_ln_g=conv_ln_g, conv_ln_b=conv_ln_b, w_branch=w_branch,
                  w_out=w_out, norm_ffn2=norm_ffn2, w_ffn2_in=w_ffn2_in, w_ffn2_out=w_ffn2_out)
    depth = w_in.shape[0]
    bp, tp, d = x_prompt.shape
    bs, ts, _ = x_sample.shape
    past_len = page_table.shape[1] * PAGE
    n_pool = cache_k.shape[1]

    rope_p = _rope_table(0, tp)
    cos_s, sin_s, _, _ = _rope_table(past_len, 8)
    cos_s = jnp.broadcast_to(cos_s[0:1], (bs, HEAD_W))
    sin_s = jnp.broadcast_to(sin_s[0:1], (bs, HEAD_W))
    ck = cache_k.reshape(depth, n_pool, PAGE * N_HEAD, HEAD_W)
    cv = cache_v.reshape(depth, n_pool, PAGE * N_HEAD, HEAD_W)
    page_flat = page_table.reshape(-1)

    xp = x_prompt.reshape(bp * tp, d)
    xs = x_sample.reshape(bs * ts, d)
    outs = [[] for _ in range(8)]
    kv_p = None
    for l in range(depth):
        lam_init = 0.8 - 0.6 * math.exp(-0.3 * l)
        w = _layer_weights(l, params)
        final = (norm_final, l == depth - 1)
        xp, kv_p, s_p, gc_p, c_p = _prompt_layer(xp, w, rope_p, bp, lam_init, l, depth, kv_p, final)
        xs, k_s, v_s, s_s, gc_s, c_s = _sample_layer(xs, w, cos_s, sin_s, ck, cv, page_flat, l,
                                                     state_gdn[l], state_gdn_conv[l], state_conv[l], lam_init, final)
        vals = (k_s.reshape(bs, ts, N_HEAD, HEAD_W), v_s.reshape(bs, ts, N_HEAD, HEAD_W),
                s_p, s_s, gc_p, gc_s, c_p, c_s)
        for lst, val in zip(outs, vals):
            lst.append(val)
    y_prompt = xp.reshape(bp, tp, d)
    y_sample = xs.reshape(bs, ts, d)
    k_prompt, v_prompt = (a.reshape(depth, bp, tp, N_HEAD, HEAD_W) for a in kv_p)
    return (y_prompt, y_sample, k_prompt, v_prompt) + tuple(jnp.stack(o) for o in outs)
```
